```python
import math
import jax
import jax.numpy as jnp
from jax import lax
import numpy as np

D_MODEL = 1024
BATCH = 8
SEQ = 2048
DEPTH = 2
DEC_BATCH = 32
DEC_SEQ = 8
PAST_LEN = 8192
PAGE_SIZE = 128

N_A_LAYERS = DEPTH // 2
N_B_LAYERS = DEPTH - N_A_LAYERS
H_A = 8
DK_A = D_MODEL // (2 * H_A)
DV_A = D_MODEL // H_A
QK_W = H_A * DK_A
V_W = H_A * DV_A
IN_A_W = 2 * QK_W + 2 * V_W + 2 * H_A
CHUNK = 64
H_B = 8
HD_B = D_MODEL // H_B
QB_W = H_B * HD_B
BLOCK = 256
TOPK = 3
Q_BLOCK = 64
D_FF = 4 * D_MODEL
ALPHA = (2.0 * DEPTH) ** 0.25
BETA = (8.0 * DEPTH) ** -0.25
LN_EPS = 1e-5
F32 = jnp.float32

kernel_name = "yoco_mlstm_moba_decoder_step"


def layer_norm(x, g, b):
    xf = x.astype(F32)
    mu = xf.mean(-1, keepdims=True)
    var = jnp.square(xf - mu).mean(-1, keepdims=True)
    return ((xf - mu) * lax.rsqrt(var + LN_EPS) * g.astype(F32) + b.astype(F32)).astype(x.dtype)


def sq_relu_mlp(x, w_up, w_down):
    return jnp.square(jax.nn.relu(x @ w_up)) @ w_down


def alibi_slopes(n):
    return 2.0 ** (-8.0 * jnp.arange(1, n + 1, dtype=F32) / n)


def mlstm_chunkwise(q, k, v, ig, lf, C0, n0, m0):
    B, L, H, DK = q.shape
    c = math.gcd(L, CHUNK)
    nc = L // c

    def to_chunks(a):
        a = a.reshape((B, nc, c, H) + a.shape[3:])
        return jnp.swapaxes(jnp.moveaxis(a, 1, 0), 2, 3)

    causal = jnp.tril(jnp.ones((c, c), dtype=bool))

    def step(carry, xs):
        C, n, m = carry
        qc, kc, vc, ic, fc = xs
        b = jnp.cumsum(fc, axis=-1)
        dmat = jnp.where(causal, b[..., :, None] - b[..., None, :] + ic[..., None, :], -jnp.inf)
        inter = b + m[..., None]
        m_t = jnp.maximum(inter, dmat.max(-1))
        w = jnp.exp(dmat - m_t[..., None])
        a = jnp.exp(inter - m_t)
        s = jnp.einsum('bhtd,bhsd->bhts', qc, kc) * w
        num = jnp.einsum('bhts,bhsv->bhtv', s, vc) + a[..., None] * jnp.einsum('bhtd,bhdv->bhtv', qc, C)
        den = s.sum(-1) + a * jnp.einsum('bhtd,bhd->bht', qc, n)
        h = num / jnp.maximum(jnp.abs(den), jnp.exp(-m_t))[..., None]
        g = b[..., -1:] - b + ic
        inter_end = b[..., -1] + m
        m_new = jnp.maximum(inter_end, g.max(-1))
        ws = jnp.exp(g - m_new[..., None])
        ae = jnp.exp(inter_end - m_new)
        C_new = ae[..., None, None] * C + jnp.einsum('bhs,bhsd,bhsv->bhdv', ws, kc, vc)
        n_new = ae[..., None] * n + jnp.einsum('bhs,bhsd->bhd', ws, kc)
        return (C_new, n_new, m_new), h

    carry0 = (C0.astype(F32), n0.astype(F32), m0.astype(F32))
    (C, n, m), h = lax.scan(step, carry0, (to_chunks(q), to_chunks(k), to_chunks(v), to_chunks(ig), to_chunks(lf)))
    h = jnp.moveaxis(jnp.swapaxes(h, 2, 3), 0, 1).reshape(B, L, H, v.shape[-1])
    return h, (C, n, m)


def mlstm_mixer(x, w_in, b_gate, w_out, C0, n0, m0):
    B, L, _ = x.shape
    proj = x @ w_in
    q, k, v, o, g = jnp.split(proj, [QK_W, 2 * QK_W, 2 * QK_W + V_W, 2 * QK_W + 2 * V_W], axis=-1)
    q = q.reshape(B, L, H_A, DK_A).astype(F32)
    k = k.reshape(B, L, H_A, DK_A).astype(F32) * (DK_A ** -0.5)
    v = v.reshape(B, L, H_A, DV_A).astype(F32)
    g = g.astype(F32) + b_gate.astype(F32)
    ig = g[..., :H_A]
    lf = jax.nn.log_sigmoid(g[..., H_A:])
    h, state = mlstm_chunkwise(q, k, v, ig, lf, C0, n0, m0)
    h = jax.nn.sigmoid(o.astype(F32)) * h.reshape(B, L, V_W)
    return h.astype(x.dtype) @ w_out, state


def moba_select(q, means, pos_q):
    nb = means.shape[1]
    own = pos_q // BLOCK
    scores = jnp.einsum('bqhd,bnhd->bqhn', q.astype(F32), means)
    past = jnp.arange(nb)[None, :] < own[:, None]
    scores = jnp.where(past[None, :, None, :], scores, -jnp.inf)
    kk = min(TOPK, nb)
    _, idx = lax.top_k(scores, kk)
    valid = jnp.arange(kk)[None, :] < own[:, None]
    return idx, valid


def moba_attend(q, pos_q, sel_k, sel_v, sel_idx, sel_valid, own_k, own_v, own_pos, slopes):
    B, Q, H, D = q.shape
    nsel = sel_k.shape[3]
    scale = D ** -0.5
    lg_sel = jnp.einsum('bqhd,bqhrkd->bqhrk', q, sel_k).astype(F32) * scale
    sel_pos = sel_idx[..., None] * BLOCK + jnp.arange(BLOCK)
    dist_sel = (pos_q[None, :, None, None, None] - sel_pos).astype(F32)
    lg_sel = jnp.where(sel_valid[None, :, None, :, None],
                       lg_sel - slopes[None, None, :, None, None] * dist_sel, -jnp.inf)
    lg_own = jnp.einsum('bqhd,bkhd->bqhk', q, own_k).astype(F32) * scale
    dist_own = pos_q[:, None] - own_pos[None, :]
    lg_own = jnp.where((dist_own >= 0)[None, :, None, :],
                       lg_own - slopes[None, None, :, None] * dist_own.astype(F32)[None, :, None, :], -jnp.inf)
    p = jax.nn.softmax(jnp.concatenate([lg_sel.reshape(B, Q, H, nsel * BLOCK), lg_own], axis=-1), axis=-1)
    p_sel = p[..., :nsel * BLOCK].reshape(B, Q, H, nsel, BLOCK).astype(q.dtype)
    p_own = p[..., nsel * BLOCK:].astype(q.dtype)
    return (jnp.einsum('bqhrk,bqhrkd->bqhd', p_sel, sel_v)
            + jnp.einsum('bqhk,bkhd->bqhd', p_own, own_v))


def moba_prompt_kv_side(k, v):
    B, S, H, D = k.shape
    nb = -(-S // BLOCK)
    widths = ((0, 0), (0, nb * BLOCK - S), (0, 0), (0, 0))
    k_blk = jnp.pad(k, widths).reshape(B, nb, BLOCK, H, D)
    v_blk = jnp.pad(v, widths).reshape(B, nb, BLOCK, H, D)
    means = k_blk.astype(F32).mean(2)
    return k_blk, v_blk, means


def moba_prompt_attend(q, ctx, slopes):
    k_blk, v_blk, means = ctx
    B, S, H, D = q.shape
    k_bh = jnp.transpose(k_blk, (0, 3, 1, 2, 4))
    v_bh = jnp.transpose(v_blk, (0, 3, 1, 2, 4))
    bi = jnp.arange(B)[:, None, None, None]
    hi = jnp.arange(H)[None, None, :, None]

    def one_block(start):
        qb = lax.dynamic_slice_in_dim(q, start, Q_BLOCK, axis=1)
        pos = start + jnp.arange(Q_BLOCK)
        idx, valid = moba_select(qb, means, pos)
        sel_k = k_bh[bi, hi, idx]
        sel_v = v_bh[bi, hi, idx]
        own = start // BLOCK
        own_k = lax.dynamic_index_in_dim(k_blk, own, axis=1, keepdims=False)
        own_v = lax.dynamic_index_in_dim(v_blk, own, axis=1, keepdims=False)
        own_pos = own * BLOCK + jnp.arange(BLOCK)
        return moba_attend(qb, pos, sel_k, sel_v, idx, valid, own_k, own_v, own_pos, slopes)

    out = lax.map(one_block, jnp.arange(0, S, Q_BLOCK))
    return jnp.moveaxis(out, 0, 1).reshape(B, S, H, D)


def moba_sample_kv_side(k_new, v_new, cache_k, cache_v, page_table):
    B, L, H, D = k_new.shape
    n_pages = page_table.shape[1]
    past_len = n_pages * PAGE_SIZE
    ppb = BLOCK // PAGE_SIZE
    n_blocks = past_len // BLOCK + 1
    k_past = cache_k[page_table]
    page_sums = k_past.astype(F32).sum(2)
    page_sums = jnp.pad(page_sums, ((0, 0), (0, n_blocks * ppb - n_pages), (0, 0), (0, 0)))
    means = page_sums.reshape(B, n_blocks, ppb, H, D).sum(2) / BLOCK
    own_start = (past_len // BLOCK) * BLOCK
    r_pages = (past_len - own_start) // PAGE_SIZE
    own_k_past = k_past[:, n_pages - r_pages:].reshape(B, r_pages * PAGE_SIZE, H, D)
    own_v_past = cache_v[page_table[:, n_pages - r_pages:]].reshape(B, r_pages * PAGE_SIZE, H, D)
    own_k = jnp.concatenate([own_k_past, k_new], axis=1)
    own_v = jnp.concatenate([own_v_past, v_new], axis=1)
    own_pos = own_start + jnp.arange(r_pages * PAGE_SIZE + L)
    return means, own_k, own_v, own_pos


def moba_sample_attend(q, ctx, cache_k, cache_v, page_table, slopes):
    means, own_k, own_v, own_pos = ctx
    B, L, H, D = q.shape
    n_pages = page_table.shape[1]
    past_len = n_pages * PAGE_SIZE
    ppb = BLOCK // PAGE_SIZE
    pos_q = past_len + jnp.arange(L)
    idx, valid = moba_select(q, means, pos_q)
    r = idx.shape[-1]
    lp = jnp.minimum(idx[..., None] * ppb + jnp.arange(ppb), n_pages - 1)
    phys = page_table[jnp.arange(B)[:, None, None, None, None], lp]
    hh = jnp.arange(H)[None, None, :, None, None]
    sel_k = cache_k[phys, :, hh].reshape(B, L, H, r, BLOCK, D)
    sel_v = cache_v[phys, :, hh].reshape(B, L, H, r, BLOCK, D)
    return moba_attend(q, pos_q, sel_k, sel_v, idx, valid, own_k, own_v, own_pos, slopes)


def trunk(x, C0, n0, m0, kv_side, attend, w_in_a, b_gate_a, w_out_a, w_q_b, w_out_b, w_kv,
          w_up, w_down, ln_g, ln_b):
    B, L, _ = x.shape
    Cs, ns, ms = [], [], []
    k = v = ctx = None
    for layer in range(DEPTH):
        if layer < N_A_LAYERS:
            h, (C, n, m) = mlstm_mixer(x, w_in_a[layer], b_gate_a[layer], w_out_a[layer],
                                       C0[layer], n0[layer], m0[layer])
            Cs.append(C)
            ns.append(n)
            ms.append(m)
        else:
            if layer == N_A_LAYERS:
                kv = x @ w_kv
                k = kv[..., :QB_W].reshape(B, L, H_B, HD_B)
                v = kv[..., QB_W:].reshape(B, L, H_B, HD_B)
                ctx = kv_side(k, v)
            j = layer - N_A_LAYERS
            q = (x @ w_q_b[j]).reshape(B, L, H_B, HD_B)
            h = attend(q, ctx).reshape(B, L, QB_W) @ w_out_b[j]
        x = layer_norm(ALPHA * x + h, ln_g[layer, 0], ln_b[layer, 0])
        x = layer_norm(ALPHA * x + sq_relu_mlp(x, w_up[layer], w_down[layer]), ln_g[layer, 1], ln_b[layer, 1])
    return x, jnp.stack(Cs), jnp.stack(ns), jnp.stack(ms), k, v


def setup_inputs(seed: int = 0) -> dict:
    key = jax.random.key(seed)
    ks = jax.random.split(key, 20)
    nrm = jax.random.normal
    n_pages = PAST_LEN // PAGE_SIZE
    n_used = DEC_BATCH * n_pages
    n_pool = (5 * n_used + 3) // 4
    page_table = jax.random.permutation(ks[0], n_pool)[:n_used].reshape(DEC_BATCH, n_pages).astype(jnp.int32)
    x_prompt = nrm(ks[1], (BATCH, SEQ, D_MODEL), F32)
    x_sample = nrm(ks[2], (DEC_BATCH, DEC_SEQ, D_MODEL), F32)
    state_C = 0.5 * nrm(ks[3], (N_A_LAYERS, DEC_BATCH, H_A, DK_A, DV_A), F32)
    state_n = nrm(ks[4], (N_A_LAYERS, DEC_BATCH, H_A, DK_A), F32)
    state_m = nrm(ks[5], (N_A_LAYERS, DEC_BATCH, H_A), F32)
    cache_k = nrm(ks[6], (n_pool, PAGE_SIZE, H_B, HD_B), F32)
    cache_v = nrm(ks[7], (n_pool, PAGE_SIZE, H_B, HD_B), F32)
    in_scale = jnp.concatenate([jnp.ones((2 * QK_W,), F32), jnp.full((V_W,), BETA, F32),
                                jnp.ones((V_W + 2 * H_A,), F32)]) * (D_MODEL ** -0.5)
    w_in_a = nrm(ks[8], (N_A_LAYERS, D_MODEL, IN_A_W), F32) * in_scale
    b_gate_a = jnp.concatenate([0.1 * nrm(ks[9], (N_A_LAYERS, H_A), F32),
                                3.0 + jax.random.uniform(ks[10], (N_A_LAYERS, H_A), F32, 0.0, 3.0)], axis=-1)
    w_out_a = nrm(ks[11], (N_A_LAYERS, V_W, D_MODEL), F32) * (V_W ** -0.5 * BETA)
    w_q_b = nrm(ks[12], (N_B_LAYERS, D_MODEL, QB_W), F32) * (D_MODEL ** -0.5)
    w_out_b = nrm(ks[13], (N_B_LAYERS, QB_W, D_MODEL), F32) * (QB_W ** -0.5 * BETA)
    kv_scale = jnp.concatenate([jnp.ones((QB_W,), F32), jnp.full((QB_W,), BETA, F32)]) * (D_MODEL ** -0.5)
    w_kv = nrm(ks[14], (D_MODEL, 2 * QB_W), F32) * kv_scale
    w_up = nrm(ks[15], (DEPTH, D_MODEL, D_FF), F32) * (D_MODEL ** -0.5)
    w_down = nrm(ks[16], (DEPTH, D_FF, D_MODEL), F32) * (D_FF ** -0.5 * BETA)
    ln_g = 1.0 + 0.02 * nrm(ks[17], (DEPTH, 2, D_MODEL), F32)
    ln_b = 0.02 * nrm(ks[18], (DEPTH, 2, D_MODEL), F32)
    return {"x_prompt": x_prompt, "x_sample": x_sample, "state_C": state_C, "state_n": state_n,
            "state_m": state_m, "cache_k": cache_k, "cache_v": cache_v, "page_table": page_table,
            "w_in_a": w_in_a, "b_gate_a": b_gate_a, "w_out_a": w_out_a, "w_q_b": w_q_b,
            "w_out_b": w_out_b, "w_kv": w_kv, "w_up": w_up, "w_down": w_down,
            "ln_g": ln_g, "ln_b": ln_b}


def reference(x_prompt, x_sample, state_C, state_n, state_m, cache_k, cache_v, page_table,
              w_in_a, b_gate_a, w_out_a, w_q_b, w_out_b, w_kv, w_up, w_down, ln_g, ln_b):
    slopes = alibi_slopes(H_B)
    bp = x_prompt.shape[0]
    zC = jnp.zeros((N_A_LAYERS, bp, H_A, DK_A, DV_A), F32)
    zn = jnp.zeros((N_A_LAYERS, bp, H_A, DK_A), F32)
    zm = jnp.zeros((N_A_LAYERS, bp, H_A), F32)
    y_prompt, p_C, p_n, p_m, p_k, p_v = trunk(
        x_prompt, zC, zn, zm, moba_prompt_kv_side,
        lambda q, ctx: moba_prompt_attend(q, ctx, slopes),
        w_in_a, b_gate_a, w_out_a, w_q_b, w_out_b, w_kv, w_up, w_down, ln_g, ln_b)
    y_sample, s_C, s_n, s_m, s_k, s_v = trunk(
        x_sample, state_C, state_n, state_m,
        lambda k, v: moba_sample_kv_side(k, v, cache_k, cache_v, page_table),
        lambda q, ctx: moba_sample_attend(q, ctx, cache_k, cache_v, page_table, slopes),
        w_in_a, b_gate_a, w_out_a, w_q_b, w_out_b, w_kv, w_up, w_down, ln_g, ln_b)
    return (y_prompt, y_sample, p_C, p_n, p_m, p_k, p_v, s_C, s_n, s_m, s_k, s_v)
```

```python
import functools

import numpy as np
import jax
import jax.numpy as jnp
from jax import lax
from jax.experimental import pallas as pl
from jax.experimental.pallas import tpu as pltpu

F32 = jnp.float32
BF16 = jnp.bfloat16

D_MODEL = 1024
N_HEADS = 8
DK_A = 64
DV_A = 128
QK_W = N_HEADS * DK_A
V_W = N_HEADS * DV_A
HD_B = 128
BLOCK = 256
TOPK = 3
PAGE = 128
D_FF = 4 * D_MODEL
DEPTH = 2
ALPHA = (2.0 * DEPTH) ** 0.25
LN_EPS = 1e-5
NEG_INF = float("-inf")

LANES = 128
VMEM_LIMIT = 56 * 1024 * 1024

MLSTM_CHUNK = 64


def _cparams(n_axes):
    return pltpu.CompilerParams(dimension_semantics=("arbitrary",) * n_axes,
                                vmem_limit_bytes=VMEM_LIMIT)


def _dot(a, b):
    return jnp.dot(a, b, preferred_element_type=F32)


def _dot_nt(a, b):
    return lax.dot_general(a, b, (((1,), (1,)), ((), ())), preferred_element_type=F32)


def _dot_tn(a, b):
    return lax.dot_general(a, b, (((0,), (0,)), ((), ())), preferred_element_type=F32)


def _split3(x):
    x1 = x.astype(BF16)
    r1 = x - x1.astype(F32)
    x2 = r1.astype(BF16)
    x3 = (r1 - x2.astype(F32)).astype(BF16)
    return x1, x2, x3


def _layer_norm(y, g, b):
    mu = jnp.mean(y, axis=-1, keepdims=True)
    d = y - mu
    var = jnp.mean(d * d, axis=-1, keepdims=True)
    return d * lax.rsqrt(var + LN_EPS) * g + b


def _proj_a_kernel(x_ref, wm_ref, wg_ref, bg_ref, qkv_ref, o_ref, g_ref):
    x = x_ref[...]
    xh = x.astype(BF16)
    nq = 2 * QK_W + V_W
    step = 512
    for c0 in range(0, nq, step):
        acc = _dot(xh, wm_ref[:, c0:c0 + step])
        if QK_W <= c0 < 2 * QK_W:
            acc = acc * (DK_A ** -0.5)
        qkv_ref[:, c0:c0 + step] = acc.astype(BF16)
    for c0 in range(0, V_W, step):
        o_ref[:, c0:c0 + step] = _dot(xh, wm_ref[:, nq + c0:nq + c0 + step])
    xl = (x - xh.astype(F32)).astype(BF16)
    wg = wg_ref[...]
    wgh = wg.astype(BF16)
    wgl = (wg - wgh.astype(F32)).astype(BF16)
    g = _dot(xh, wgh) + _dot(xl, wgh) + _dot(xh, wgl) + bg_ref[...]
    lane = lax.broadcasted_iota(jnp.int32, g.shape, 1)
    log_f = jnp.minimum(g, 0.0) - jnp.log1p(jnp.exp(-jnp.abs(g)))
    g_ref[...] = jnp.where(lane < N_HEADS, g, jnp.where(lane < 2 * N_HEADS, log_f, 0.0))


def _proj_a(x, wm, wg, bg, tm):
    rows = x.shape[0]
    return pl.pallas_call(
        _proj_a_kernel,
        grid=(rows // tm,),
        in_specs=[pl.BlockSpec((tm, D_MODEL), lambda i: (i, 0)),
                  pl.BlockSpec(wm.shape, lambda i: (0, 0)),
                  pl.BlockSpec(wg.shape, lambda i: (0, 0)),
                  pl.BlockSpec(bg.shape, lambda i: (0, 0))],
        out_specs=[pl.BlockSpec((tm, 2 * QK_W + V_W), lambda i: (i, 0)),
                   pl.BlockSpec((tm, V_W), lambda i: (i, 0)),
                   pl.BlockSpec((tm, LANES), lambda i: (i, 0))],
        out_shape=[jax.ShapeDtypeStruct((rows, 2 * QK_W + V_W), BF16),
                   jax.ShapeDtypeStruct((rows, V_W), F32),
                   jax.ShapeDtypeStruct((rows, LANES), F32)],
        compiler_params=_cparams(1),
        name="mlstm_in_proj",
    )(x, wm, wg, bg)


def _mlstm_kernel(qkv_ref, o_ref, g_ref, x_ref, c0_ref, m0_ref, wout_ref, lng_ref, lnb_ref,
                  y_ref, ct_ref, mt_ref, caug_s, m_s, h_s, *, tc, ch):
    t = pl.program_id(1)

    @pl.when(t == 0)
    def _():
        caug_s[...] = c0_ref[0]
        m_s[...] = m0_ref[0]

    row = lax.broadcasted_iota(jnp.int32, (ch, ch), 0)
    col = lax.broadcasted_iota(jnp.int32, (ch, ch), 1)
    tril = row >= col
    eye = row == col
    tri_b = jnp.where(tril, 1.0, 0.0).astype(BF16)
    ones_blk = jnp.where(lax.broadcasted_iota(jnp.int32, (ch, LANES), 1) == 0, 1.0, 0.0).astype(BF16)
    glane = lax.broadcasted_iota(jnp.int32, (ch, LANES), 1)

    for j in range(tc // ch):
        r0 = j * ch
        gates = g_ref[r0:r0 + ch, :]
        lf = jnp.where(glane >= N_HEADS, gates, 0.0)
        l1, l2, l3 = _split3(lf)
        bc_all = _dot(tri_b, l1) + _dot(tri_b, l2) + _dot(tri_b, l3)
        for h in range(N_HEADS):
            bcol = bc_all[:, N_HEADS + h:N_HEADS + h + 1]
            icol = gates[:, h:h + 1]
            m_prev = m_s[h:h + 1, 0:1]
            brow = jnp.sum(jnp.where(eye, jnp.broadcast_to(bcol, (ch, ch)), 0.0), axis=0, keepdims=True)
            irow = jnp.sum(jnp.where(eye, jnp.broadcast_to(icol, (ch, ch)), 0.0), axis=0, keepdims=True)
            dmat = jnp.where(tril, bcol - brow + irow, NEG_INF)
            inter = bcol + m_prev
            m_t = jnp.maximum(inter, jnp.max(dmat, axis=1, keepdims=True))
            w = jnp.exp(dmat - m_t)
            a = jnp.exp(inter - m_t)
            q = qkv_ref[r0:r0 + ch, h * DK_A:(h + 1) * DK_A]
            k = qkv_ref[r0:r0 + ch, QK_W + h * DK_A:QK_W + (h + 1) * DK_A]
            v = qkv_ref[r0:r0 + ch, 2 * QK_W + h * DV_A:2 * QK_W + (h + 1) * DV_A]
            sw = _dot_nt(q, k) * w
            qc = _dot(q, caug_s[h].astype(BF16))
            num = _dot(sw.astype(BF16), v) + a * qc[:, :DV_A]
            den = jnp.sum(sw, axis=1, keepdims=True) + a * qc[:, DV_A:DV_A + 1]
            hh = num / jnp.maximum(jnp.abs(den), jnp.exp(-m_t))
            og = o_ref[r0:r0 + ch, h * DV_A:(h + 1) * DV_A]
            hh = hh * (1.0 / (1.0 + jnp.exp(-og)))
            h_s[r0:r0 + ch, h * DV_A:(h + 1) * DV_A] = hh.astype(BF16)
            b_end = bcol[ch - 1:ch, :]
            gcol = b_end - bcol + icol
            ie = b_end + m_prev
            m_new = jnp.maximum(ie, jnp.max(gcol, axis=0, keepdims=True))
            ws = jnp.exp(gcol - m_new)
            ae = jnp.exp(ie - m_new)
            kw = (ws * k.astype(F32)).astype(BF16)
            vaug = jnp.concatenate([v, ones_blk], axis=1)
            caug_s[h] = ae * caug_s[h] + _dot_tn(kw, vaug)
            m_s[h:h + 1, :] = jnp.broadcast_to(m_new, (1, LANES))

    y = _dot(h_s[...], wout_ref[...])
    y_ref[...] = _layer_norm(ALPHA * x_ref[...] + y, lng_ref[...], lnb_ref[...])

    @pl.when(t == pl.num_programs(1) - 1)
    def _():
        ct_ref[0] = caug_s[...]
        mt_ref[0] = m_s[...]


def _mlstm(qkv, o, g, x, c0, m0, wout, lng, lnb, *, batch, seq, tc, ch):
    n_t = seq // tc
    rows = batch * seq
    kern = functools.partial(_mlstm_kernel, tc=tc, ch=ch)
    rmap = lambda b, t: (b * n_t + t, 0)
    cmap = lambda b, t: (0, 0)
    return pl.pallas_call(
        kern,
        grid=(batch, n_t),
        in_specs=[pl.BlockSpec((tc, qkv.shape[1]), rmap),
                  pl.BlockSpec((tc, V_W), rmap),
                  pl.BlockSpec((tc, LANES), rmap),
                  pl.BlockSpec((tc, D_MODEL), rmap),
                  pl.BlockSpec((1, N_HEADS, DK_A, 2 * DV_A), lambda b, t: (b, 0, 0, 0)),
                  pl.BlockSpec((1, N_HEADS, LANES), lambda b, t: (b, 0, 0)),
                  pl.BlockSpec(wout.shape, cmap),
                  pl.BlockSpec(lng.shape, cmap),
                  pl.BlockSpec(lnb.shape, cmap)],
        out_specs=[pl.BlockSpec((tc, D_MODEL), rmap),
                   pl.BlockSpec((1, N_HEADS, DK_A, 2 * DV_A), lambda b, t: (b, 0, 0, 0)),
                   pl.BlockSpec((1, N_HEADS, LANES), lambda b, t: (b, 0, 0))],
        out_shape=[jax.ShapeDtypeStruct((rows, D_MODEL), F32),
                   jax.ShapeDtypeStruct((batch, N_HEADS, DK_A, 2 * DV_A), F32),
                   jax.ShapeDtypeStruct((batch, N_HEADS, LANES), F32)],
        scratch_shapes=[pltpu.VMEM((N_HEADS, DK_A, 2 * DV_A), F32),
                        pltpu.VMEM((N_HEADS, LANES), F32),
                        pltpu.VMEM((tc, V_W), BF16)],
        compiler_params=_cparams(2),
        name="mlstm_scan",
    )(qkv, o, g, x, c0, m0, wout, lng, lnb)


def _mlp_kernel(x_ref, wup_ref, wdn_ref, lng_ref, lnb_ref, y_ref, acc_s, xb_s):
    j = pl.program_id(1)

    @pl.when(j == 0)
    def _():
        xb_s[...] = x_ref[...].astype(BF16)
        acc_s[...] = jnp.zeros_like(acc_s)

    hid = jnp.maximum(_dot(xb_s[...], wup_ref[...]), 0.0)
    acc_s[...] += _dot((hid * hid).astype(BF16), wdn_ref[...])

    @pl.when(j == pl.num_programs(1) - 1)
    def _():
        y_ref[...] = _layer_norm(ALPHA * x_ref[...] + acc_s[...], lng_ref[...], lnb_ref[...])


def _mlp(x, wup, wdn, lng, lnb, tm, tf):
    rows = x.shape[0]
    return pl.pallas_call(
        _mlp_kernel,
        grid=(rows // tm, D_FF // tf),
        in_specs=[pl.BlockSpec((tm, D_MODEL), lambda i, j: (i, 0)),
                  pl.BlockSpec((D_MODEL, tf), lambda i, j: (0, j)),
                  pl.BlockSpec((tf, D_MODEL), lambda i, j: (j, 0)),
                  pl.BlockSpec(lng.shape, lambda i, j: (0, 0)),
                  pl.BlockSpec(lnb.shape, lambda i, j: (0, 0))],
        out_specs=pl.BlockSpec((tm, D_MODEL), lambda i, j: (i, 0)),
        out_shape=jax.ShapeDtypeStruct((rows, D_MODEL), F32),
        scratch_shapes=[pltpu.VMEM((tm, D_MODEL), F32), pltpu.VMEM((tm, D_MODEL), BF16)],
        compiler_params=_cparams(2),
        name="mlp_ln",
    )(x, wup, wdn, lng, lnb)


def _qkv_b_kernel(x_ref, w_ref, q_ref, k_ref, v_ref, kb_ref, vb_ref, mean_ref, *, tm):
    xh = x_ref[...].astype(BF16)
    step = 512
    for c0 in range(0, D_MODEL, step):
        q_ref[:, c0:c0 + step] = _dot(xh, w_ref[:, c0:c0 + step]).astype(BF16)
        kk = _dot(xh, w_ref[:, D_MODEL + c0:D_MODEL + c0 + step])
        k_ref[:, c0:c0 + step] = kk
        kb_ref[:, c0:c0 + step] = kk.astype(BF16)
        for r in range(tm // BLOCK):
            mean_ref[r, :, c0:c0 + step] = jnp.mean(kk[r * BLOCK:(r + 1) * BLOCK], axis=0, keepdims=True)
        vv = _dot(xh, w_ref[:, 2 * D_MODEL + c0:2 * D_MODEL + c0 + step])
        v_ref[:, c0:c0 + step] = vv
        vb_ref[:, c0:c0 + step] = vv.astype(BF16)


def _qkv_b(x, w, tm):
    rows = x.shape[0]
    rmap = lambda i: (i, 0)
    rspec = pl.BlockSpec((tm, D_MODEL), rmap)
    return pl.pallas_call(
        functools.partial(_qkv_b_kernel, tm=tm),
        grid=(rows // tm,),
        in_specs=[rspec, pl.BlockSpec(w.shape, lambda i: (0, 0))],
        out_specs=[rspec, rspec, rspec, rspec, rspec,
                   pl.BlockSpec((tm // BLOCK, 1, D_MODEL), lambda i: (i, 0, 0))],
        out_shape=[jax.ShapeDtypeStruct((rows, D_MODEL), BF16),
                   jax.ShapeDtypeStruct((rows, D_MODEL), F32),
                   jax.ShapeDtypeStruct((rows, D_MODEL), F32),
                   jax.ShapeDtypeStruct((rows, D_MODEL), BF16),
                   jax.ShapeDtypeStruct((rows, D_MODEL), BF16),
                   jax.ShapeDtypeStruct((rows // BLOCK, 1, D_MODEL), F32)],
        compiler_params=_cparams(1),
        name="moba_qkv_proj",
    )(x, w)


def _out_ln_kernel(a_ref, w_ref, x_ref, lng_ref, lnb_ref, y_ref):
    y = _dot(a_ref[...].astype(BF16), w_ref[...])
    y_ref[...] = _layer_norm(ALPHA * x_ref[...] + y, lng_ref[...], lnb_ref[...])


def _out_ln(a, w, x, lng, lnb, tm):
    rows = x.shape[0]
    rspec = pl.BlockSpec((tm, D_MODEL), lambda i: (i, 0))
    cmap = lambda i: (0, 0)
    return pl.pallas_call(
        _out_ln_kernel,
        grid=(rows // tm,),
        in_specs=[rspec, pl.BlockSpec(w.shape, cmap), rspec,
                  pl.BlockSpec(lng.shape, cmap), pl.BlockSpec(lnb.shape, cmap)],
        out_specs=rspec,
        out_shape=jax.ShapeDtypeStruct((rows, D_MODEL), F32),
        compiler_params=_cparams(1),
        name="attn_out_ln",
    )(a, w, x, lng, lnb)


def _head_block_diag(rows_8):
    tiled = jnp.concatenate([rows_8] * (LANES // 8), axis=0)
    r = lax.broadcasted_iota(jnp.int32, (LANES, D_MODEL), 0)
    c = lax.broadcasted_iota(jnp.int32, (LANES, D_MODEL), 1)
    keep = (r // 8) == (c // HD_B)
    return jnp.where(keep, tiled, 0.0).astype(BF16)


def _moba_prompt_kernel(qi_ref, kn_ref, q_ref, kb_ref, vb_ref, mean_ref, o_ref,
                        selb_s, m_s, l_s, acc_s):
    p = pl.program_id(1)
    qi = qi_ref[p]
    kn = kn_ref[p]
    own = kn == qi
    scale = HD_B ** -0.5

    @pl.when(own)
    def _():
        mbd = _head_block_diag(mean_ref[0])
        sc = _dot_nt(q_ref[...], mbd)
        lane = lax.broadcasted_iota(jnp.int32, sc.shape, 1)
        blk = lane % 8
        past = jnp.logical_and(lane < N_HEADS * 8, blk < qi)
        sm = jnp.where(past, sc, NEG_INF)
        rank = jnp.zeros(sc.shape, F32)
        for r in range(1, 8):
            lo = pltpu.roll(sm, r, 1)
            rank += jnp.where(blk >= r, jnp.where(lo >= sm, 1.0, 0.0), 0.0)
            hi = pltpu.roll(sm, LANES - r, 1)
            rank += jnp.where(blk + r < 8, jnp.where(hi > sm, 1.0, 0.0), 0.0)
        selb_s[...] = jnp.where(past, jnp.where(rank < TOPK, 1.0, 0.0), 0.0)
        m_s[...] = jnp.full(m_s.shape, NEG_INF, F32)
        l_s[...] = jnp.zeros_like(l_s)
        acc_s[...] = jnp.zeros_like(acc_s)

    er = lax.broadcasted_iota(jnp.int32, (LANES, LANES), 0)
    ec = lax.broadcasted_iota(jnp.int32, (LANES, LANES), 1)
    pick = jnp.where((er == ec * 8 + kn) & (ec < N_HEADS), 1.0, 0.0).astype(BF16)
    selcols = _dot(selb_s[...].astype(BF16), pick)
    colbias = jnp.where(jnp.logical_or(selcols > 0.5, own), 0.0, NEG_INF)

    t_i = lax.broadcasted_iota(jnp.int32, (BLOCK, BLOCK), 0)
    j_i = lax.broadcasted_iota(jnp.int32, (BLOCK, BLOCK), 1)
    dist = (t_i - j_i + (qi - kn) * BLOCK).astype(F32)
    dist = jnp.where(jnp.logical_and(own, j_i > t_i), float("inf"), dist)

    for h in range(N_HEADS):
        cs = slice(h * HD_B, (h + 1) * HD_B)
        s = _dot_nt(q_ref[:, cs], kb_ref[:, cs])
        lg = s * scale - (2.0 ** -(h + 1)) * dist + colbias[:, h:h + 1]
        m_old = m_s[h]
        m_new = jnp.maximum(m_old, jnp.max(lg, axis=1, keepdims=True))
        alpha = jnp.exp(m_old - m_new)
        pr = jnp.exp(lg - m_new)
        l_s[h] = alpha * l_s[h] + jnp.sum(pr, axis=1, keepdims=True)
        acc_s[:, cs] = alpha * acc_s[:, cs] + _dot(pr.astype(BF16), vb_ref[:, cs])
        m_s[h] = m_new

    last = jnp.logical_or(qi == 0, kn == qi - 1)

    @pl.when(last)
    def _():
        for h in range(N_HEADS):
            cs = slice(h * HD_B, (h + 1) * HD_B)
            o_ref[:, cs] = (acc_s[:, cs] / l_s[h]).astype(BF16)


def _moba_prompt(q, kb, vb, means, *, batch, seq):
    nb = seq // BLOCK
    qi_l, kn_l = [], []
    for i in range(nb):
        qi_l += [i] * (i + 1)
        kn_l += [i] + list(range(i))
    qi_arr = jnp.asarray(np.array(qi_l, np.int32))
    kn_arr = jnp.asarray(np.array(kn_l, np.int32))
    blk = (BLOCK, D_MODEL)
    grid_spec = pltpu.PrefetchScalarGridSpec(
        num_scalar_prefetch=2,
        grid=(batch, len(qi_l)),
        in_specs=[pl.BlockSpec(blk, lambda b, p, qi, kn: (b * nb + qi[p], 0)),
                  pl.BlockSpec(blk, lambda b, p, qi, kn: (b * nb + kn[p], 0)),
                  pl.BlockSpec(blk, lambda b, p, qi, kn: (b * nb + kn[p], 0)),
                  pl.BlockSpec((1, nb, D_MODEL), lambda b, p, qi, kn: (b, 0, 0))],
        out_specs=pl.BlockSpec(blk, lambda b, p, qi, kn: (b * nb + qi[p], 0)),
        scratch_shapes=[pltpu.VMEM((BLOCK, LANES), F32),
                        pltpu.VMEM((N_HEADS, BLOCK, 1), F32),
                        pltpu.VMEM((N_HEADS, BLOCK, 1), F32),
                        pltpu.VMEM((BLOCK, D_MODEL), F32)])
    return pl.pallas_call(
        _moba_prompt_kernel,
        grid_spec=grid_spec,
        out_shape=jax.ShapeDtypeStruct((batch * seq, D_MODEL), BF16),
        compiler_params=_cparams(2),
        name="moba_prompt_attn",
    )(qi_arr, kn_arr, q, kb, vb, means)


PAGES_PER_STEP = 8


def _moba_sample_kernel(pt_ref, q_ref, kn_ref, vn_ref, slope_ref, *refs, n_pages, dec_seq):
    kp_refs = refs[:PAGES_PER_STEP]
    vp_refs = refs[PAGES_PER_STEP:2 * PAGES_PER_STEP]
    o_ref = refs[2 * PAGES_PER_STEP]
    qbd_s, lg_s, p_s, po_s, acc_s, den_s = refs[2 * PAGES_PER_STEP + 1:]
    ph = pl.program_id(1)
    s = pl.program_id(2)
    n_steps = pl.num_programs(2)
    past_len = n_pages * PAGE
    n_blocks = past_len // BLOCK
    scale = HD_B ** -0.5

    @pl.when(jnp.logical_and(ph == 0, s == 0))
    def _():
        qbd_s[...] = _head_block_diag(q_ref[...])

    @pl.when(ph == 0)
    def _():
        for i in range(PAGES_PER_STEP):
            kpage = kp_refs[i][0].astype(BF16)
            r0 = pl.multiple_of((s * PAGES_PER_STEP + i) * PAGE, PAGE)
            lg_s[pl.ds(r0, PAGE), :] = _dot_nt(kpage, qbd_s[...])

    @pl.when(jnp.logical_and(ph == 0, s == n_steps - 1))
    def _():
        lg3 = lg_s[...].reshape(n_blocks, BLOCK, LANES)
        sc = jnp.sum(lg3, axis=1) * (1.0 / BLOCK)
        ridx = lax.broadcasted_iota(jnp.int32, sc.shape, 0)
        rank = jnp.zeros(sc.shape, F32)
        for n in range(n_blocks):
            a = sc[n:n + 1, :]
            rank += jnp.where(ridx > n, jnp.where(a >= sc, 1.0, 0.0), jnp.where(a > sc, 1.0, 0.0))
        bias = jnp.where(rank < TOPK, 0.0, NEG_INF)
        slope = slope_ref[...]
        lane = lax.broadcasted_iota(jnp.int32, (1, LANES), 1)
        qpos = (lane % 8 + past_len).astype(F32)
        kpos = (lax.broadcasted_iota(jnp.int32, (n_blocks, BLOCK, LANES), 0) * BLOCK
                + lax.broadcasted_iota(jnp.int32, (n_blocks, BLOCK, LANES), 1)).astype(F32)
        lgs = lg3 * scale - slope[None] * (qpos[None] - kpos) + bias[:, None, :]
        knew = jnp.concatenate([kn_ref[...], jnp.zeros((PAGE - dec_seq, D_MODEL), F32)], axis=0)
        lgo = _dot_nt(knew.astype(BF16), qbd_s[...])
        ki = lax.broadcasted_iota(jnp.int32, (PAGE, LANES), 0)
        li = lax.broadcasted_iota(jnp.int32, (PAGE, LANES), 1) % 8
        lgo = jnp.where(ki <= li, lgo * scale - slope * (li - ki).astype(F32), NEG_INF)
        mx = jnp.maximum(jnp.max(jnp.max(lgs, axis=1), axis=0, keepdims=True),
                         jnp.max(lgo, axis=0, keepdims=True))
        p_s[...] = jnp.exp(lgs - mx[None]).reshape(past_len, LANES).astype(BF16)
        po_s[...] = jnp.exp(lgo - mx).astype(BF16)

    @pl.when(jnp.logical_and(ph == 1, s == 0))
    def _():
        vnew = jnp.concatenate([vn_ref[...], jnp.zeros((PAGE - dec_seq, D_MODEL), F32)], axis=0)
        po = po_s[...]
        acc_s[...] = _dot_tn(po, vnew.astype(BF16))
        den_s[...] = _dot_tn(po, jnp.ones((PAGE, LANES), BF16))

    @pl.when(ph == 1)
    def _():
        for i in range(PAGES_PER_STEP):
            r0 = pl.multiple_of((s * PAGES_PER_STEP + i) * PAGE, PAGE)
            pp = p_s[pl.ds(r0, PAGE), :]
            acc_s[...] += _dot_tn(pp, vp_refs[i][0].astype(BF16))
            den_s[...] += _dot_tn(pp, jnp.ones((PAGE, LANES), BF16))

    @pl.when(jnp.logical_and(ph == 1, s == n_steps - 1))
    def _():
        for h in range(N_HEADS):
            rs = slice(h * 8, h * 8 + dec_seq)
            cs = slice(h * HD_B, (h + 1) * HD_B)
            o_ref[:, cs] = acc_s[rs, cs] / den_s[rs, :]


def _moba_sample(page_table, q, k_new, v_new, cache_k, cache_v, *, batch, dec_seq):
    n_pages = page_table.shape[1]
    n_steps = n_pages // PAGES_PER_STEP
    past_len = n_pages * PAGE
    slopes = np.zeros((1, LANES), np.float32)
    slopes[0, :N_HEADS * 8] = np.repeat(2.0 ** -np.arange(1, N_HEADS + 1, dtype=np.float64), 8)
    rspec = pl.BlockSpec((dec_seq, D_MODEL), lambda b, ph, s, pt: (b, 0))

    def kmap(i):
        return lambda b, ph, s, pt: (pt[b, jnp.where(ph == 0, s, n_steps - 1) * PAGES_PER_STEP + i], 0, 0)

    def vmap_(i):
        return lambda b, ph, s, pt: (pt[b, jnp.where(ph == 0, 0, s) * PAGES_PER_STEP + i], 0, 0)

    pspec = lambda m: pl.BlockSpec((1, PAGE, D_MODEL), m)
    grid_spec = pltpu.PrefetchScalarGridSpec(
        num_scalar_prefetch=1,
        grid=(batch, 2, n_steps),
        in_specs=[rspec, rspec, rspec, pl.BlockSpec((1, LANES), lambda b, ph, s, pt: (0, 0))]
                 + [pspec(kmap(i)) for i in range(PAGES_PER_STEP)]
                 + [pspec(vmap_(i)) for i in range(PAGES_PER_STEP)],
        out_specs=rspec,
        scratch_shapes=[pltpu.VMEM((LANES, D_MODEL), BF16),
                        pltpu.VMEM((past_len, LANES), F32),
                        pltpu.VMEM((past_len, LANES), BF16),
                        pltpu.VMEM((PAGE, LANES), BF16),
                        pltpu.VMEM((LANES, D_MODEL), F32),
                        pltpu.VMEM((LANES, LANES), F32)])
    kern = functools.partial(_moba_sample_kernel, n_pages=n_pages, dec_seq=dec_seq)
    return pl.pallas_call(
        kern,
        grid_spec=grid_spec,
        out_shape=jax.ShapeDtypeStruct((batch * dec_seq, D_MODEL), F32),
        compiler_params=_cparams(3),
        name="moba_sample_attn",
    )(page_table, q, k_new, v_new, jnp.asarray(slopes),
      *([cache_k] * PAGES_PER_STEP), *([cache_v] * PAGES_PER_STEP))


def kernel(x_prompt, x_sample, state_C, state_n, state_m, cache_k, cache_v, page_table,
           w_in_a, b_gate_a, w_out_a, w_q_b, w_out_b, w_kv, w_up, w_down, ln_g, ln_b):
    bp, sp, _ = x_prompt.shape
    bs, ss, _ = x_sample.shape
    ch = MLSTM_CHUNK
    nq = 2 * QK_W + 2 * V_W

    wm_a = w_in_a[0, :, :nq].astype(BF16)
    wg_a = jnp.pad(w_in_a[0, :, nq:], ((0, 0), (0, LANES - 2 * N_HEADS)))
    bg_a = jnp.pad(b_gate_a[0], (0, LANES - 2 * N_HEADS)).reshape(1, LANES)
    wout_a = w_out_a[0].astype(BF16)
    w_qkv_b = jnp.concatenate([w_q_b[0], w_kv], axis=1).astype(BF16)
    wout_b = w_out_b[0].astype(BF16)
    wup = w_up.astype(BF16)
    wdn = w_down.astype(BF16)
    lng = ln_g.reshape(DEPTH, 2, 1, D_MODEL)
    lnb = ln_b.reshape(DEPTH, 2, 1, D_MODEL)

    xp = x_prompt.reshape(bp * sp, D_MODEL)
    qkv, o, g = _proj_a(xp, wm_a, wg_a, bg_a, 512)
    c0 = jnp.zeros((bp, N_HEADS, DK_A, 2 * DV_A), F32)
    m0 = jnp.zeros((bp, N_HEADS, LANES), F32)
    x1, caug_p, m_p = _mlstm(qkv, o, g, xp, c0, m0, wout_a, lng[0, 0], lnb[0, 0],
                             batch=bp, seq=sp, tc=256, ch=ch)
    x2 = _mlp(x1, wup[0], wdn[0], lng[0, 1], lnb[0, 1], 1024, 512)
    q, k, v, kb, vb, means = _qkv_b(x2, w_qkv_b, 512)
    att = _moba_prompt(q, kb, vb, means.reshape(bp, sp // BLOCK, D_MODEL), batch=bp, seq=sp)
    x3 = _out_ln(att, wout_b, x2, lng[1, 0], lnb[1, 0], 512)
    y_prompt = _mlp(x3, wup[1], wdn[1], lng[1, 1], lnb[1, 1], 1024, 512)

    rows_s = bs * ss
    xs = x_sample.reshape(rows_s, D_MODEL)
    qkv_s, o_s, g_s = _proj_a(xs, wm_a, wg_a, bg_a, rows_s)
    pad3 = lambda a, val=0.0: jnp.pad(a.reshape(bs, ss, -1), ((0, 0), (0, ch - ss), (0, 0)),
                                      constant_values=val).reshape(bs * ch, -1)
    lane = jnp.arange(LANES)
    g_pad = jnp.where(lane[None, :] < N_HEADS, pad3(g_s, NEG_INF), pad3(g_s, 0.0))
    c0s = jnp.concatenate([state_C[0], state_n[0][..., None],
                           jnp.zeros((bs, N_HEADS, DK_A, DV_A - 1), F32)], axis=-1)
    m0s = jnp.broadcast_to(state_m[0][..., None], (bs, N_HEADS, LANES))
    x1s_pad, caug_s, m_s = _mlstm(pad3(qkv_s), pad3(o_s), g_pad, pad3(xs), c0s, m0s, wout_a,
                                  lng[0, 0], lnb[0, 0], batch=bs, seq=ch, tc=ch, ch=ch)
    x1s = x1s_pad.reshape(bs, ch, D_MODEL)[:, :ss].reshape(rows_s, D_MODEL)
    x2s = _mlp(x1s, wup[0], wdn[0], lng[0, 1], lnb[0, 1], rows_s, 512)
    q_s, k_s, v_s, _, _, _ = _qkv_b(x2s, w_qkv_b, rows_s)
    n_pool = cache_k.shape[0]
    att_s = _moba_sample(page_table, q_s.astype(F32), k_s, v_s,
                         cache_k.reshape(n_pool, PAGE, D_MODEL), cache_v.reshape(n_pool, PAGE, D_MODEL),
                         batch=bs, dec_seq=ss)
    x3s = _out_ln(att_s, wout_b, x2s, lng[1, 0], lnb[1, 0], rows_s)
    y_sample = _mlp(x3s, wup[1], wdn[1], lng[1, 1], lnb[1, 1], rows_s, 512)

    def split_state(caug, m):
        return (caug[None, :, :, :, :DV_A], caug[None, :, :, :, DV_A], m[None, :, :, 0])

    p_c, p_n, p_m = split_state(caug_p, m_p)
    s_c, s_n, s_m = split_state(caug_s, m_s)
    return (y_prompt.reshape(bp, sp, D_MODEL), y_sample.reshape(bs, ss, D_MODEL),
            p_c, p_n, p_m,
            k.reshape(bp, sp, N_HEADS, HD_B), v.reshape(bp, sp, N_HEADS, HD_B),
            s_c, s_n, s_m,
            k_s.reshape(bs, ss, N_HEADS, HD_B), v_s.reshape(bs, ss, N_HEADS, HD_B))
```

```python
import functools

import numpy as np
import jax
import jax.numpy as jnp
from jax import lax
from jax.experimental import pallas as pl
from jax.experimental.pallas import tpu as pltpu

F32 = jnp.float32
BF16 = jnp.bfloat16

D_MODEL = 1024
N_HEADS = 8
DK_A = 64
DV_A = 128
QK_W = N_HEADS * DK_A
V_W = N_HEADS * DV_A
HD_B = 128
BLOCK = 256
TOPK = 3
PAGE = 128
D_FF = 4 * D_MODEL
DEPTH = 2
ALPHA = (2.0 * DEPTH) ** 0.25
LN_EPS = 1e-5
NEG_INF = float("-inf")

LANES = 128
VMEM_LIMIT = 56 * 1024 * 1024

MLSTM_CHUNK = 128


def _cparams(n_axes):
    return pltpu.CompilerParams(dimension_semantics=("arbitrary",) * n_axes,
                                vmem_limit_bytes=VMEM_LIMIT)


def _dot(a, b):
    return jnp.dot(a, b, preferred_element_type=F32)


def _dot_nt(a, b):
    return lax.dot_general(a, b, (((1,), (1,)), ((), ())), preferred_element_type=F32)


def _dot_tn(a, b):
    return lax.dot_general(a, b, (((0,), (0,)), ((), ())), preferred_element_type=F32)


def _split3(x):
    x1 = x.astype(BF16)
    r1 = x - x1.astype(F32)
    x2 = r1.astype(BF16)
    x3 = (r1 - x2.astype(F32)).astype(BF16)
    return x1, x2, x3


def _layer_norm(y, g, b):
    mu = jnp.mean(y, axis=-1, keepdims=True)
    d = y - mu
    var = jnp.mean(d * d, axis=-1, keepdims=True)
    return d * lax.rsqrt(var + LN_EPS) * g + b


def _cummax_rows(x):
    row = lax.broadcasted_iota(jnp.int32, x.shape, 0)
    shift = 1
    while shift < x.shape[0]:
        x = jnp.maximum(x, jnp.where(row >= shift, pltpu.roll(x, shift, 0), NEG_INF))
        shift *= 2
    return x


def _proj_a_kernel(x_ref, wm_ref, wg_ref, bg_ref, qkv_ref, o_ref, gi_ref, gf_ref, gt_ref, *, ch):
    x = x_ref[...]
    xh = x.astype(BF16)
    nq = 2 * QK_W + V_W
    step = 512
    for c0 in range(0, nq, step):
        acc = _dot(xh, wm_ref[:, c0:c0 + step])
        if QK_W <= c0 < 2 * QK_W:
            acc = acc * (DK_A ** -0.5)
        qkv_ref[:, c0:c0 + step] = acc.astype(BF16)
    for c0 in range(0, V_W, step):
        o_ref[:, c0:c0 + step] = _dot(xh, wm_ref[:, nq + c0:nq + c0 + step])
    xl = (x - xh.astype(F32)).astype(BF16)
    wg = wg_ref[...]
    wgh = wg.astype(BF16)
    wgl = (wg - wgh.astype(F32)).astype(BF16)
    g = _dot(xh, wgh) + _dot(xl, wgh) + _dot(xh, wgl) + bg_ref[...]
    lane = lax.broadcasted_iota(jnp.int32, g.shape, 1)
    log_f = jnp.minimum(g, 0.0) - jnp.log1p(jnp.exp(-jnp.abs(g)))
    gi_ref[...] = jnp.where(lane < N_HEADS, g, 0.0)
    gf_ref[...] = jnp.where(lane < N_HEADS, pltpu.roll(log_f, LANES - N_HEADS, 1), 0.0)
    both_t = jnp.where(lane < N_HEADS, g, jnp.where(lane < 2 * N_HEADS, log_f, 0.0)).T
    for c in range(x.shape[0] // ch):
        gt_ref[c] = both_t[0:2 * N_HEADS, c * ch:(c + 1) * ch]


def _proj_a(x, wm, wg, bg, tm, ch):
    rows = x.shape[0]
    return pl.pallas_call(
        functools.partial(_proj_a_kernel, ch=ch),
        grid=(rows // tm,),
        in_specs=[pl.BlockSpec((tm, D_MODEL), lambda i: (i, 0)),
                  pl.BlockSpec(wm.shape, lambda i: (0, 0)),
                  pl.BlockSpec(wg.shape, lambda i: (0, 0)),
                  pl.BlockSpec(bg.shape, lambda i: (0, 0))],
        out_specs=[pl.BlockSpec((tm, 2 * QK_W + V_W), lambda i: (i, 0)),
                   pl.BlockSpec((tm, V_W), lambda i: (i, 0)),
                   pl.BlockSpec((tm, LANES), lambda i: (i, 0)),
                   pl.BlockSpec((tm, LANES), lambda i: (i, 0)),
                   pl.BlockSpec((tm // ch, 2 * N_HEADS, ch), lambda i: (i, 0, 0))],
        out_shape=[jax.ShapeDtypeStruct((rows, 2 * QK_W + V_W), BF16),
                   jax.ShapeDtypeStruct((rows, V_W), F32),
                   jax.ShapeDtypeStruct((rows, LANES), F32),
                   jax.ShapeDtypeStruct((rows, LANES), F32),
                   jax.ShapeDtypeStruct((rows // ch, 2 * N_HEADS, ch), F32)],
        compiler_params=_cparams(1),
        name="mlstm_in_proj",
    )(x, wm, wg, bg)


def _mlstm_kernel(qkv_ref, o_ref, gi_ref, gf_ref, gt_ref, x_ref, c0_ref, m0_ref, wout_ref,
                  lng_ref, lnb_ref, y_ref, ct_ref, mt_ref, caug_s, m_s, h_s, *, tc, ch):
    t = pl.program_id(1)

    @pl.when(t == 0)
    def _():
        caug_s[...] = c0_ref[0]
        m_s[...] = m0_ref[0]

    row = lax.broadcasted_iota(jnp.int32, (ch, ch), 0)
    col = lax.broadcasted_iota(jnp.int32, (ch, ch), 1)
    tril = row >= col
    tri_b = jnp.where(tril, 1.0, 0.0).astype(BF16)
    tri_t = jnp.where(row <= col, 1.0, 0.0).astype(BF16)
    ones_blk = jnp.where(lax.broadcasted_iota(jnp.int32, (ch, LANES), 1) == 0, 1.0, 0.0).astype(BF16)
    trow = lax.broadcasted_iota(jnp.int32, (2 * N_HEADS, ch), 0)

    for j in range(tc // ch):
        r0 = j * ch
        gi = gi_ref[r0:r0 + ch, :]
        f1, f2, f3 = _split3(gf_ref[r0:r0 + ch, :])
        b = _dot(tri_b, f1) + _dot(tri_b, f2) + _dot(tri_b, f3)
        u = gi - b
        m_prev = m_s[...]
        big_m = jnp.maximum(_cummax_rows(u), m_prev)
        a_all = jnp.exp(m_prev - big_m)
        e_all = jnp.exp(-(b + big_m))
        m_end = big_m[ch - 1:ch, :]
        ws_all = jnp.exp(u - m_end)
        ae_all = a_all[ch - 1:ch, :]
        gt = gt_ref[j]
        r1, r2, r3 = _split3(jnp.where(trow >= N_HEADS, gt, 0.0))
        b_t = _dot(r1, tri_t) + _dot(r2, tri_t) + _dot(r3, tri_t)
        u_t = gt[0:N_HEADS, :] - b_t[N_HEADS:2 * N_HEADS, :]
        for h in range(N_HEADS):
            w = jnp.exp(jnp.where(tril, u_t[h:h + 1, :] - big_m[:, h:h + 1], NEG_INF))
            q = qkv_ref[r0:r0 + ch, h * DK_A:(h + 1) * DK_A]
            k = qkv_ref[r0:r0 + ch, QK_W + h * DK_A:QK_W + (h + 1) * DK_A]
            v = qkv_ref[r0:r0 + ch, 2 * QK_W + h * DV_A:2 * QK_W + (h + 1) * DV_A]
            sw = _dot_nt(q, k) * w
            qc = _dot(q, caug_s[h].astype(BF16))
            a = a_all[:, h:h + 1]
            num = _dot(sw.astype(BF16), v) + a * qc[:, :DV_A]
            den = jnp.sum(sw, axis=1, keepdims=True) + a * qc[:, DV_A:DV_A + 1]
            rden = 1.0 / jnp.maximum(jnp.abs(den), e_all[:, h:h + 1])
            og = o_ref[r0:r0 + ch, h * DV_A:(h + 1) * DV_A]
            hh = num * rden * (1.0 / (1.0 + jnp.exp(-og)))
            h_s[r0:r0 + ch, h * DV_A:(h + 1) * DV_A] = hh.astype(BF16)
            kw = (ws_all[:, h:h + 1] * k.astype(F32)).astype(BF16)
            vaug = jnp.concatenate([v, ones_blk], axis=1)
            caug_s[h] = ae_all[:, h:h + 1] * caug_s[h] + _dot_tn(kw, vaug)
        m_s[...] = b[ch - 1:ch, :] + m_end

    y = _dot(h_s[...], wout_ref[...])
    y_ref[...] = _layer_norm(ALPHA * x_ref[...] + y, lng_ref[...], lnb_ref[...])

    @pl.when(t == pl.num_programs(1) - 1)
    def _():
        ct_ref[0] = caug_s[...]
        mt_ref[0] = m_s[...]


def _mlstm(qkv, o, gi, gf, gt, x, c0, m0, wout, lng, lnb, *, batch, seq, tc, ch):
    n_t = seq // tc
    rows = batch * seq
    kern = functools.partial(_mlstm_kernel, tc=tc, ch=ch)
    rmap = lambda b, t: (b * n_t + t, 0)
    cmap = lambda b, t: (0, 0)
    smap = lambda b, t: (b, 0, 0, 0)
    mmap = lambda b, t: (b, 0, 0)
    return pl.pallas_call(
        kern,
        grid=(batch, n_t),
        in_specs=[pl.BlockSpec((tc, qkv.shape[1]), rmap),
                  pl.BlockSpec((tc, V_W), rmap),
                  pl.BlockSpec((tc, LANES), rmap),
                  pl.BlockSpec((tc, LANES), rmap),
                  pl.BlockSpec((tc // ch, 2 * N_HEADS, ch), lambda b, t: (b * n_t + t, 0, 0)),
                  pl.BlockSpec((tc, D_MODEL), rmap),
                  pl.BlockSpec((1, N_HEADS, DK_A, 2 * DV_A), smap),
                  pl.BlockSpec((1, 1, LANES), mmap),
                  pl.BlockSpec(wout.shape, cmap),
                  pl.BlockSpec(lng.shape, cmap),
                  pl.BlockSpec(lnb.shape, cmap)],
        out_specs=[pl.BlockSpec((tc, D_MODEL), rmap),
                   pl.BlockSpec((1, N_HEADS, DK_A, 2 * DV_A), smap),
                   pl.BlockSpec((1, 1, LANES), mmap)],
        out_shape=[jax.ShapeDtypeStruct((rows, D_MODEL), F32),
                   jax.ShapeDtypeStruct((batch, N_HEADS, DK_A, 2 * DV_A), F32),
                   jax.ShapeDtypeStruct((batch, 1, LANES), F32)],
        scratch_shapes=[pltpu.VMEM((N_HEADS, DK_A, 2 * DV_A), F32),
                        pltpu.VMEM((1, LANES), F32),
                        pltpu.VMEM((tc, V_W), BF16)],
        compiler_params=_cparams(2),
        name="mlstm_scan",
    )(qkv, o, gi, gf, gt, x, c0, m0, wout, lng, lnb)


def _mlp_kernel(x_ref, wup_ref, wdn_ref, lng_ref, lnb_ref, y_ref, acc_s, xb_s):
    j = pl.program_id(1)

    @pl.when(j == 0)
    def _():
        xb_s[...] = x_ref[...].astype(BF16)
        acc_s[...] = jnp.zeros_like(acc_s)

    hid = jnp.maximum(_dot(xb_s[...], wup_ref[...]), 0.0)
    acc_s[...] += _dot((hid * hid).astype(BF16), wdn_ref[...])

    @pl.when(j == pl.num_programs(1) - 1)
    def _():
        y_ref[...] = _layer_norm(ALPHA * x_ref[...] + acc_s[...], lng_ref[...], lnb_ref[...])


def _mlp(x, wup, wdn, lng, lnb, tm, tf):
    rows = x.shape[0]
    return pl.pallas_call(
        _mlp_kernel,
        grid=(rows // tm, D_FF // tf),
        in_specs=[pl.BlockSpec((tm, D_MODEL), lambda i, j: (i, 0)),
                  pl.BlockSpec((D_MODEL, tf), lambda i, j: (0, j)),
                  pl.BlockSpec((tf, D_MODEL), lambda i, j: (j, 0)),
                  pl.BlockSpec(lng.shape, lambda i, j: (0, 0)),
                  pl.BlockSpec(lnb.shape, lambda i, j: (0, 0))],
        out_specs=pl.BlockSpec((tm, D_MODEL), lambda i, j: (i, 0)),
        out_shape=jax.ShapeDtypeStruct((rows, D_MODEL), F32),
        scratch_shapes=[pltpu.VMEM((tm, D_MODEL), F32), pltpu.VMEM((tm, D_MODEL), BF16)],
        compiler_params=_cparams(2),
        name="mlp_ln",
    )(x, wup, wdn, lng, lnb)


def _qkv_b_kernel(x_ref, w_ref, q_ref, k_ref, v_ref, kb_ref, vb_ref, mean_ref, *, tm):
    xh = x_ref[...].astype(BF16)
    step = 512
    for c0 in range(0, D_MODEL, step):
        q_ref[:, c0:c0 + step] = _dot(xh, w_ref[:, c0:c0 + step]).astype(BF16)
        kk = _dot(xh, w_ref[:, D_MODEL + c0:D_MODEL + c0 + step])
        k_ref[:, c0:c0 + step] = kk
        kb_ref[:, c0:c0 + step] = kk.astype(BF16)
        for r in range(tm // BLOCK):
            mean_ref[r, :, c0:c0 + step] = jnp.mean(kk[r * BLOCK:(r + 1) * BLOCK], axis=0, keepdims=True)
        vv = _dot(xh, w_ref[:, 2 * D_MODEL + c0:2 * D_MODEL + c0 + step])
        v_ref[:, c0:c0 + step] = vv
        vb_ref[:, c0:c0 + step] = vv.astype(BF16)


def _qkv_b(x, w, tm):
    rows = x.shape[0]
    rmap = lambda i: (i, 0)
    rspec = pl.BlockSpec((tm, D_MODEL), rmap)
    return pl.pallas_call(
        functools.partial(_qkv_b_kernel, tm=tm),
        grid=(rows // tm,),
        in_specs=[rspec, pl.BlockSpec(w.shape, lambda i: (0, 0))],
        out_specs=[rspec, rspec, rspec, rspec, rspec,
                   pl.BlockSpec((tm // BLOCK, 1, D_MODEL), lambda i: (i, 0, 0))],
        out_shape=[jax.ShapeDtypeStruct((rows, D_MODEL), BF16),
                   jax.ShapeDtypeStruct((rows, D_MODEL), F32),
                   jax.ShapeDtypeStruct((rows, D_MODEL), F32),
                   jax.ShapeDtypeStruct((rows, D_MODEL), BF16),
                   jax.ShapeDtypeStruct((rows, D_MODEL), BF16),
                   jax.ShapeDtypeStruct((rows // BLOCK, 1, D_MODEL), F32)],
        compiler_params=_cparams(1),
        name="moba_qkv_proj",
    )(x, w)


def _out_ln_kernel(a_ref, w_ref, x_ref, lng_ref, lnb_ref, y_ref):
    y = _dot(a_ref[...].astype(BF16), w_ref[...])
    y_ref[...] = _layer_norm(ALPHA * x_ref[...] + y, lng_ref[...], lnb_ref[...])


def _out_ln(a, w, x, lng, lnb, tm):
    rows = x.shape[0]
    rspec = pl.BlockSpec((tm, D_MODEL), lambda i: (i, 0))
    cmap = lambda i: (0, 0)
    return pl.pallas_call(
        _out_ln_kernel,
        grid=(rows // tm,),
        in_specs=[rspec, pl.BlockSpec(w.shape, cmap), rspec,
                  pl.BlockSpec(lng.shape, cmap), pl.BlockSpec(lnb.shape, cmap)],
        out_specs=rspec,
        out_shape=jax.ShapeDtypeStruct((rows, D_MODEL), F32),
        compiler_params=_cparams(1),
        name="attn_out_ln",
    )(a, w, x, lng, lnb)


def _head_slopes(rep):
    slopes = np.zeros((1, LANES), np.float32)
    slopes[0, :N_HEADS * rep] = np.repeat(2.0 ** -np.arange(1, N_HEADS + 1, dtype=np.float64), rep)
    return jnp.asarray(slopes)


def _head_block_diag(rows_8):
    tiled = jnp.concatenate([rows_8] * (LANES // 8), axis=0)
    r = lax.broadcasted_iota(jnp.int32, (LANES, D_MODEL), 0)
    c = lax.broadcasted_iota(jnp.int32, (LANES, D_MODEL), 1)
    keep = (r // 8) == (c // HD_B)
    return jnp.where(keep, tiled, 0.0).astype(BF16)


def _moba_prompt_kernel(qi_ref, kn_ref, q_ref, kb_ref, vb_ref, mean_ref, slope_ref, o_ref,
                        sel_s, mask_s, m_s, l_s, acc_s):
    p = pl.program_id(1)
    qi = qi_ref[p]
    kn = kn_ref[p]
    own = kn == qi
    scale = HD_B ** -0.5

    @pl.when(jnp.logical_and(pl.program_id(0) == 0, p == 0))
    def _():
        t_i = lax.broadcasted_iota(jnp.int32, (BLOCK, BLOCK), 0)
        j_i = lax.broadcasted_iota(jnp.int32, (BLOCK, BLOCK), 1)
        jf = j_i.astype(F32)
        for h in range(N_HEADS):
            mask_s[0, h] = (2.0 ** -(h + 1)) * jf
            mask_s[1, h] = jnp.where(j_i > t_i, NEG_INF, (2.0 ** -(h + 1)) * jf)

    @pl.when(own)
    def _():
        mbd = _head_block_diag(mean_ref[0])
        sc = _dot_nt(q_ref[...], mbd)
        lane = lax.broadcasted_iota(jnp.int32, sc.shape, 1)
        blk = lane % 8
        past = jnp.logical_and(lane < N_HEADS * 8, blk < qi)
        sm = jnp.where(past, sc, NEG_INF)
        rank = jnp.zeros(sc.shape, F32)
        for r in range(1, 8):
            lo = pltpu.roll(sm, r, 1)
            rank += jnp.where(blk >= r, jnp.where(lo >= sm, 1.0, 0.0), 0.0)
            hi = pltpu.roll(sm, LANES - r, 1)
            rank += jnp.where(blk + r < 8, jnp.where(hi > sm, 1.0, 0.0), 0.0)
        sel_s[...] = jnp.where(past, jnp.where(rank < TOPK, 1.0, 0.0), 0.0)
        m_s[...] = jnp.full(m_s.shape, NEG_INF, F32)
        l_s[...] = jnp.zeros_like(l_s)
        acc_s[...] = jnp.zeros_like(acc_s)

    er = lax.broadcasted_iota(jnp.int32, (LANES, LANES), 0)
    ec = lax.broadcasted_iota(jnp.int32, (LANES, LANES), 1)
    pick = jnp.where(jnp.logical_and(er == ec * 8 + kn, ec < N_HEADS), 1.0, 0.0).astype(BF16)
    selcols = _dot(sel_s[...].astype(BF16), pick)
    own_i = own.astype(jnp.int32)
    kbase = slope_ref[...] * (kn * BLOCK).astype(F32)
    biasc = jnp.where(selcols + own_i.astype(F32) > 0.5, kbase, NEG_INF)
    ones_v = jnp.ones((BLOCK, HD_B), BF16)
    rows = BLOCK

    for h in range(N_HEADS):
        cs = slice(h * HD_B, (h + 1) * HD_B)
        vaug = jnp.concatenate([vb_ref[:, cs], ones_v], axis=1)
        for r0 in range(0, BLOCK, rows):
            rs = slice(r0, r0 + rows)
            bias = jnp.broadcast_to(biasc[rs, h:h + 1], (rows, LANES))
            s = _dot_nt(q_ref[rs, cs], kb_ref[:, cs])
            lg = s * scale + mask_s[own_i, h, rs, :] + jnp.concatenate([bias, bias], axis=1)
            m_old = m_s[h, rs, :]
            m_new = jnp.maximum(m_old, jnp.max(lg, axis=1, keepdims=True))
            alpha = jnp.exp(m_old - m_new)
            pr = jnp.exp(lg - jnp.concatenate([m_new, m_new], axis=1))
            pv = _dot(pr.astype(BF16), vaug)
            l_s[h, rs, :] = alpha * l_s[h, rs, :] + pv[:, HD_B:]
            acc_s[rs, cs] = alpha * acc_s[rs, cs] + pv[:, :HD_B]
            m_s[h, rs, :] = m_new

    last = jnp.logical_or(qi == 0, kn == qi - 1)

    @pl.when(last)
    def _():
        for h in range(N_HEADS):
            cs = slice(h * HD_B, (h + 1) * HD_B)
            o_ref[:, cs] = (acc_s[:, cs] / l_s[h]).astype(BF16)


def _moba_prompt(q, kb, vb, means, *, batch, seq):
    nb = seq // BLOCK
    qi_l, kn_l = [], []
    for i in range(nb):
        qi_l += [i] * (i + 1)
        kn_l += [i] + list(range(i))
    qi_arr = jnp.asarray(np.array(qi_l, np.int32))
    kn_arr = jnp.asarray(np.array(kn_l, np.int32))
    blk = (BLOCK, D_MODEL)
    grid_spec = pltpu.PrefetchScalarGridSpec(
        num_scalar_prefetch=2,
        grid=(batch, len(qi_l)),
        in_specs=[pl.BlockSpec(blk, lambda b, p, qi, kn: (b * nb + qi[p], 0)),
                  pl.BlockSpec(blk, lambda b, p, qi, kn: (b * nb + kn[p], 0)),
                  pl.BlockSpec(blk, lambda b, p, qi, kn: (b * nb + kn[p], 0)),
                  pl.BlockSpec((1, nb, D_MODEL), lambda b, p, qi, kn: (b, 0, 0)),
                  pl.BlockSpec((1, LANES), lambda b, p, qi, kn: (0, 0))],
        out_specs=pl.BlockSpec(blk, lambda b, p, qi, kn: (b * nb + qi[p], 0)),
        scratch_shapes=[pltpu.VMEM((BLOCK, LANES), F32),
                        pltpu.VMEM((2, N_HEADS, BLOCK, BLOCK), F32),
                        pltpu.VMEM((N_HEADS, BLOCK, LANES), F32),
                        pltpu.VMEM((N_HEADS, BLOCK, LANES), F32),
                        pltpu.VMEM((BLOCK, D_MODEL), F32)])
    return pl.pallas_call(
        _moba_prompt_kernel,
        grid_spec=grid_spec,
        out_shape=jax.ShapeDtypeStruct((batch * seq, D_MODEL), BF16),
        compiler_params=_cparams(2),
        name="moba_prompt_attn",
    )(qi_arr, kn_arr, q, kb, vb, means, _head_slopes(1))


PAGES_PER_STEP = 8


def _moba_sample_kernel(pt_ref, q_ref, kn_ref, vn_ref, slope_ref, *refs, n_pages, dec_seq):
    kp_refs = refs[:PAGES_PER_STEP]
    vp_refs = refs[PAGES_PER_STEP:2 * PAGES_PER_STEP]
    o_ref = refs[2 * PAGES_PER_STEP]
    qpad_s, lg_s, p_s, rden_s, acc_s = refs[2 * PAGES_PER_STEP + 1:]
    ph = pl.program_id(1)
    s = pl.program_id(2)
    n_steps = pl.num_programs(2)
    past_len = n_pages * PAGE
    n_blocks = past_len // BLOCK
    scale = HD_B ** -0.5

    def head_rows(page_ref, h):
        return page_ref[0, pl.ds(h, PAGE, stride=N_HEADS), :].astype(BF16)

    def pad_page(new_ref, h):
        blk = new_ref[:, h * HD_B:(h + 1) * HD_B]
        return jnp.concatenate([blk, jnp.zeros((PAGE - dec_seq, HD_B), F32)], axis=0).astype(BF16)

    def logits_t(k_of_head):
        out = _dot_nt(k_of_head(0), qpad_s[0])
        for h in range(1, N_HEADS):
            out += _dot_nt(k_of_head(h), qpad_s[h])
        return out

    @pl.when(jnp.logical_and(ph == 0, s == 0))
    def _():
        for h in range(N_HEADS):
            parts = []
            if h > 0:
                parts.append(jnp.zeros((h * 8, HD_B), F32))
            parts.append(q_ref[:, h * HD_B:(h + 1) * HD_B])
            parts.append(jnp.zeros((LANES - h * 8 - dec_seq, HD_B), F32))
            qpad_s[h] = jnp.concatenate(parts, axis=0).astype(BF16)

    @pl.when(ph == 0)
    def _():
        for i in range(PAGES_PER_STEP):
            r0 = pl.multiple_of((s * PAGES_PER_STEP + i) * PAGE, PAGE)
            lg_s[pl.ds(r0, PAGE), :] = logits_t(lambda h: head_rows(kp_refs[i], h))

    @pl.when(jnp.logical_and(ph == 0, s == n_steps - 1))
    def _():
        lg3 = lg_s[...].reshape(n_blocks, BLOCK, LANES)
        sc = jnp.sum(lg3, axis=1) * (1.0 / BLOCK)
        ridx = lax.broadcasted_iota(jnp.int32, sc.shape, 0)
        rank = jnp.zeros(sc.shape, F32)
        for n in range(n_blocks):
            a = sc[n:n + 1, :]
            rank += jnp.where(ridx > n, jnp.where(a >= sc, 1.0, 0.0), jnp.where(a > sc, 1.0, 0.0))
        slope = slope_ref[...]
        nblk = lax.broadcasted_iota(jnp.int32, (n_blocks, LANES), 0)
        blk_term = jnp.where(rank < TOPK, slope * (nblk * BLOCK - past_len).astype(F32), NEG_INF)
        row_term = slope * lax.broadcasted_iota(jnp.int32, (BLOCK, LANES), 0).astype(F32)
        lgs = lg3 * scale + blk_term[:, None, :] + row_term[None]
        lgo = logits_t(lambda h: pad_page(kn_ref, h))
        ki = lax.broadcasted_iota(jnp.int32, (PAGE, LANES), 0)
        li = lax.broadcasted_iota(jnp.int32, (PAGE, LANES), 1) % 8
        lgo = jnp.where(ki <= li, lgo * scale + slope * ki.astype(F32), NEG_INF)
        mx = jnp.maximum(jnp.max(jnp.max(lgs, axis=1), axis=0, keepdims=True),
                         jnp.max(lgo, axis=0, keepdims=True))
        pr = jnp.exp(lgs - mx[None])
        po = jnp.exp(lgo - mx)
        den = jnp.sum(jnp.sum(pr, axis=1), axis=0, keepdims=True) + jnp.sum(po, axis=0, keepdims=True)
        p_s[0:past_len, :] = pr.reshape(past_len, LANES).astype(BF16)
        p_s[past_len:past_len + PAGE, :] = po.astype(BF16)
        eye = (lax.broadcasted_iota(jnp.int32, (LANES, LANES), 0)
               == lax.broadcasted_iota(jnp.int32, (LANES, LANES), 1))
        rcol = jnp.sum(jnp.where(eye, jnp.broadcast_to(1.0 / den, (LANES, LANES)), 0.0), axis=1, keepdims=True)
        rden_s[...] = jnp.broadcast_to(rcol, (LANES, LANES))

    def accumulate(p_page, v_of_head):
        for h in range(N_HEADS):
            res = _dot_tn(p_page, v_of_head(h))
            acc_s[:, h * HD_B:(h + 1) * HD_B] += res[h * 8:h * 8 + dec_seq, :]

    @pl.when(jnp.logical_and(ph == 1, s == 0))
    def _():
        acc_s[...] = jnp.zeros_like(acc_s)
        accumulate(p_s[past_len:past_len + PAGE, :], lambda h: pad_page(vn_ref, h))

    @pl.when(ph == 1)
    def _():
        for i in range(PAGES_PER_STEP):
            r0 = pl.multiple_of((s * PAGES_PER_STEP + i) * PAGE, PAGE)
            accumulate(p_s[pl.ds(r0, PAGE), :], lambda h: head_rows(vp_refs[i], h))

    @pl.when(jnp.logical_and(ph == 1, s == n_steps - 1))
    def _():
        for h in range(N_HEADS):
            cs = slice(h * HD_B, (h + 1) * HD_B)
            o_ref[:, cs] = acc_s[:, cs] * rden_s[h * 8:h * 8 + dec_seq, :]


def _moba_sample(page_table, q, k_new, v_new, cache_k, cache_v, *, batch, dec_seq):
    n_pages = page_table.shape[1]
    n_steps = n_pages // PAGES_PER_STEP
    past_len = n_pages * PAGE
    rspec = pl.BlockSpec((dec_seq, D_MODEL), lambda b, ph, s, pt: (b, 0))

    def kmap(i):
        return lambda b, ph, s, pt: (pt[b, jnp.where(ph == 0, s, n_steps - 1) * PAGES_PER_STEP + i], 0, 0)

    def vmap_(i):
        return lambda b, ph, s, pt: (pt[b, jnp.where(ph == 0, 0, s) * PAGES_PER_STEP + i], 0, 0)

    pspec = lambda m: pl.BlockSpec((1, PAGE * N_HEADS, HD_B), m)
    grid_spec = pltpu.PrefetchScalarGridSpec(
        num_scalar_prefetch=1,
        grid=(batch, 2, n_steps),
        in_specs=[rspec, rspec, rspec, pl.BlockSpec((1, LANES), lambda b, ph, s, pt: (0, 0))]
                 + [pspec(kmap(i)) for i in range(PAGES_PER_STEP)]
                 + [pspec(vmap_(i)) for i in range(PAGES_PER_STEP)],
        out_specs=rspec,
        scratch_shapes=[pltpu.VMEM((N_HEADS, LANES, HD_B), BF16),
                        pltpu.VMEM((past_len, LANES), F32),
                        pltpu.VMEM((past_len + PAGE, LANES), BF16),
                        pltpu.VMEM((LANES, LANES), F32),
                        pltpu.VMEM((dec_seq, D_MODEL), F32)])
    kern = functools.partial(_moba_sample_kernel, n_pages=n_pages, dec_seq=dec_seq)
    return pl.pallas_call(
        kern,
        grid_spec=grid_spec,
        out_shape=jax.ShapeDtypeStruct((batch * dec_seq, D_MODEL), F32),
        compiler_params=_cparams(3),
        name="moba_sample_attn",
    )(page_table, q, k_new, v_new, _head_slopes(8),
      *([cache_k] * PAGES_PER_STEP), *([cache_v] * PAGES_PER_STEP))


def kernel(x_prompt, x_sample, state_C, state_n, state_m, cache_k, cache_v, page_table,
           w_in_a, b_gate_a, w_out_a, w_q_b, w_out_b, w_kv, w_up, w_down, ln_g, ln_b):
    bp, sp, _ = x_prompt.shape
    bs, ss, _ = x_sample.shape
    ch = MLSTM_CHUNK
    nq = 2 * QK_W + 2 * V_W

    wm_a = w_in_a[0, :, :nq].astype(BF16)
    wg_a = jnp.pad(w_in_a[0, :, nq:], ((0, 0), (0, LANES - 2 * N_HEADS)))
    bg_a = jnp.pad(b_gate_a[0], (0, LANES - 2 * N_HEADS)).reshape(1, LANES)
    wout_a = w_out_a[0].astype(BF16)
    w_qkv_b = jnp.concatenate([w_q_b[0], w_kv], axis=1).astype(BF16)
    wout_b = w_out_b[0].astype(BF16)
    wup = w_up.astype(BF16)
    wdn = w_down.astype(BF16)
    lng = ln_g.reshape(DEPTH, 2, 1, D_MODEL)
    lnb = ln_b.reshape(DEPTH, 2, 1, D_MODEL)

    xp = x_prompt.reshape(bp * sp, D_MODEL)
    qkv, o, gi, gf, gt = _proj_a(xp, wm_a, wg_a, bg_a, 512, ch)
    c0 = jnp.zeros((bp, N_HEADS, DK_A, 2 * DV_A), F32)
    m0 = jnp.zeros((bp, 1, LANES), F32)
    x1, caug_p, m_p = _mlstm(qkv, o, gi, gf, gt, xp, c0, m0, wout_a, lng[0, 0], lnb[0, 0],
                             batch=bp, seq=sp, tc=512, ch=ch)
    x2 = _mlp(x1, wup[0], wdn[0], lng[0, 1], lnb[0, 1], 1024, 512)
    q, k, v, kb, vb, means = _qkv_b(x2, w_qkv_b, 512)
    att = _moba_prompt(q, kb, vb, means.reshape(bp, sp // BLOCK, D_MODEL), batch=bp, seq=sp)
    x3 = _out_ln(att, wout_b, x2, lng[1, 0], lnb[1, 0], 512)
    y_prompt = _mlp(x3, wup[1], wdn[1], lng[1, 1], lnb[1, 1], 1024, 512)

    rows_s = bs * ss
    xs = x_sample.reshape(rows_s, D_MODEL)
    qkv_s, o_s, gi_s, gf_s, _ = _proj_a(xs, wm_a, wg_a, bg_a, rows_s, ch)
    pad3 = lambda a, val=0.0: jnp.pad(a.reshape(bs, ss, -1), ((0, 0), (0, ch - ss), (0, 0)),
                                      constant_values=val)
    flat = lambda a: a.reshape(bs * ch, -1)
    gi_pad = jnp.where(jnp.arange(LANES)[None, None, :] < N_HEADS, pad3(gi_s, NEG_INF), 0.0)
    gf_pad = pad3(gf_s)
    gt_pad = jnp.concatenate([jnp.swapaxes(gi_pad, 1, 2)[:, :N_HEADS],
                              jnp.swapaxes(gf_pad, 1, 2)[:, :N_HEADS]], axis=1)
    c0s = jnp.concatenate([state_C[0], state_n[0][..., None],
                           jnp.zeros((bs, N_HEADS, DK_A, DV_A - 1), F32)], axis=-1)
    m0s = jnp.pad(state_m[0], ((0, 0), (0, LANES - N_HEADS))).reshape(bs, 1, LANES)
    x1s_pad, caug_s, m_s = _mlstm(flat(pad3(qkv_s)), flat(pad3(o_s)), flat(gi_pad), flat(gf_pad), gt_pad,
                                  flat(pad3(xs)), c0s, m0s, wout_a, lng[0, 0], lnb[0, 0],
                                  batch=bs, seq=ch, tc=ch, ch=ch)
    x1s = x1s_pad.reshape(bs, ch, D_MODEL)[:, :ss].reshape(rows_s, D_MODEL)
    x2s = _mlp(x1s, wup[0], wdn[0], lng[0, 1], lnb[0, 1], rows_s, 512)
    q_s, k_s, v_s, _, _, _ = _qkv_b(x2s, w_qkv_b, rows_s)
    n_pool = cache_k.shape[0]
    att_s = _moba_sample(page_table, q_s.astype(F32), k_s, v_s,
                         cache_k.reshape(n_pool, PAGE * N_HEADS, HD_B),
                         cache_v.reshape(n_pool, PAGE * N_HEADS, HD_B), batch=bs, dec_seq=ss)
    x3s = _out_ln(att_s, wout_b, x2s, lng[1, 0], lnb[1, 0], rows_s)
    y_sample = _mlp(x3s, wup[1], wdn[1], lng[1, 1], lnb[1, 1], rows_s, 512)

    def split_state(caug, m):
        return (caug[None, :, :, :, :DV_A], caug[None, :, :, :, DV_A], m[None, :, 0, :N_HEADS])

    p_c, p_n, p_m = split_state(caug_p, m_p)
    s_c, s_n, s_m = split_state(caug_s, m_s)
    return (y_prompt.reshape(bp, sp, D_MODEL), y_sample.reshape(bs, ss, D_MODEL),
            p_c, p_n, p_m,
            k.reshape(bp, sp, N_HEADS, HD_B), v.reshape(bp, sp, N_HEADS, HD_B),
            s_c, s_n, s_m,
            k_s.reshape(bs, ss, N_HEADS, HD_B), v_s.reshape(bs, ss, N_HEADS, HD_B))
```

```python
import functools

import numpy as np
import jax
import jax.numpy as jnp
from jax import lax
from jax.experimental import pallas as pl
from jax.experimental.pallas import tpu as pltpu

F32 = jnp.float32
BF16 = jnp.bfloat16

D_MODEL = 1024
N_HEADS = 8
DK_A = 64
DV_A = 128
QK_W = N_HEADS * DK_A
V_W = N_HEADS * DV_A
HD_B = 128
BLOCK = 256
TOPK = 3
PAGE = 128
D_FF = 4 * D_MODEL
DEPTH = 2
ALPHA = (2.0 * DEPTH) ** 0.25
LN_EPS = 1e-5
NEG_INF = float("-inf")

LANES = 128
VMEM_LIMIT = 56 * 1024 * 1024

MLSTM_CHUNK = 128
Q_TILE = BLOCK
M_FLOOR = -1e30


def _cparams(n_axes, flags=None):
    return pltpu.CompilerParams(dimension_semantics=("arbitrary",) * n_axes,
                                vmem_limit_bytes=VMEM_LIMIT, flags=flags)


def _dot(a, b):
    return jnp.dot(a, b, preferred_element_type=F32)


def _dot_nt(a, b):
    return lax.dot_general(a, b, (((1,), (1,)), ((), ())), preferred_element_type=F32)


def _dot_tn(a, b):
    return lax.dot_general(a, b, (((0,), (0,)), ((), ())), preferred_element_type=F32)


def _split3(x):
    x1 = x.astype(BF16)
    r1 = x - x1.astype(F32)
    x2 = r1.astype(BF16)
    x3 = (r1 - x2.astype(F32)).astype(BF16)
    return x1, x2, x3


def _layer_norm(y, g, b):
    mu = jnp.mean(y, axis=-1, keepdims=True)
    d = y - mu
    var = jnp.mean(d * d, axis=-1, keepdims=True)
    return d * lax.rsqrt(var + LN_EPS) * g + b


def _cummax_rows(x):
    row = lax.broadcasted_iota(jnp.int32, x.shape, 0)
    shift = 1
    while shift < x.shape[0]:
        x = jnp.maximum(x, jnp.where(row >= shift, pltpu.roll(x, shift, 0), NEG_INF))
        shift *= 2
    return x


def _proj_a_kernel(x_ref, wm_ref, wg_ref, bg_ref, qkv_ref, o_ref, gi_ref, gf_ref, gt_ref, *, ch):
    x = x_ref[...]
    xh = x.astype(BF16)
    nq = 2 * QK_W + V_W
    step = 512
    for c0 in range(0, nq, step):
        acc = _dot(xh, wm_ref[:, c0:c0 + step])
        if QK_W <= c0 < 2 * QK_W:
            acc = acc * (DK_A ** -0.5)
        qkv_ref[:, c0:c0 + step] = acc.astype(BF16)
    for c0 in range(0, V_W, step):
        o_ref[:, c0:c0 + step] = _dot(xh, wm_ref[:, nq + c0:nq + c0 + step])
    xl = (x - xh.astype(F32)).astype(BF16)
    wg = wg_ref[...]
    wgh = wg.astype(BF16)
    wgl = (wg - wgh.astype(F32)).astype(BF16)
    g = _dot(xh, wgh) + _dot(xl, wgh) + _dot(xh, wgl) + bg_ref[...]
    lane = lax.broadcasted_iota(jnp.int32, g.shape, 1)
    log_f = jnp.minimum(g, 0.0) - jnp.log1p(jnp.exp(-jnp.abs(g)))
    gi_ref[...] = jnp.where(lane < N_HEADS, g, 0.0)
    gf_ref[...] = jnp.where(lane < N_HEADS, pltpu.roll(log_f, LANES - N_HEADS, 1), 0.0)
    both_t = jnp.where(lane < N_HEADS, g, jnp.where(lane < 2 * N_HEADS, log_f, 0.0)).T
    for c in range(x.shape[0] // ch):
        gt_ref[c] = both_t[0:2 * N_HEADS, c * ch:(c + 1) * ch]


def _proj_a(x, wm, wg, bg, tm, ch):
    rows = x.shape[0]
    return pl.pallas_call(
        functools.partial(_proj_a_kernel, ch=ch),
        grid=(rows // tm,),
        in_specs=[pl.BlockSpec((tm, D_MODEL), lambda i: (i, 0)),
                  pl.BlockSpec(wm.shape, lambda i: (0, 0)),
                  pl.BlockSpec(wg.shape, lambda i: (0, 0)),
                  pl.BlockSpec(bg.shape, lambda i: (0, 0))],
        out_specs=[pl.BlockSpec((tm, 2 * QK_W + V_W), lambda i: (i, 0)),
                   pl.BlockSpec((tm, V_W), lambda i: (i, 0)),
                   pl.BlockSpec((tm, LANES), lambda i: (i, 0)),
                   pl.BlockSpec((tm, LANES), lambda i: (i, 0)),
                   pl.BlockSpec((tm // ch, 2 * N_HEADS, ch), lambda i: (i, 0, 0))],
        out_shape=[jax.ShapeDtypeStruct((rows, 2 * QK_W + V_W), BF16),
                   jax.ShapeDtypeStruct((rows, V_W), F32),
                   jax.ShapeDtypeStruct((rows, LANES), F32),
                   jax.ShapeDtypeStruct((rows, LANES), F32),
                   jax.ShapeDtypeStruct((rows // ch, 2 * N_HEADS, ch), F32)],
        compiler_params=_cparams(1),
        name="mlstm_in_proj",
    )(x, wm, wg, bg)


def _mlstm_kernel(qkv_ref, o_ref, gi_ref, gf_ref, gt_ref, x_ref, c0_ref, m0_ref, wout_ref,
                  lng_ref, lnb_ref, y_ref, ct_ref, mt_ref, caug_s, m_s, h_s, *, tc, ch):
    t = pl.program_id(1)

    @pl.when(t == 0)
    def _():
        caug_s[...] = c0_ref[0]
        m_s[...] = m0_ref[0]

    row = lax.broadcasted_iota(jnp.int32, (ch, ch), 0)
    col = lax.broadcasted_iota(jnp.int32, (ch, ch), 1)
    tril = row >= col
    tri_b = jnp.where(tril, 1.0, 0.0).astype(BF16)
    tri_t = jnp.where(row <= col, 1.0, 0.0).astype(BF16)
    ones_blk = jnp.where(lax.broadcasted_iota(jnp.int32, (ch, LANES), 1) == 0, 1.0, 0.0).astype(BF16)
    trow = lax.broadcasted_iota(jnp.int32, (2 * N_HEADS, ch), 0)

    for j in range(tc // ch):
        r0 = j * ch
        gi = gi_ref[r0:r0 + ch, :]
        f1, f2, f3 = _split3(gf_ref[r0:r0 + ch, :])
        b = _dot(tri_b, f1) + _dot(tri_b, f2) + _dot(tri_b, f3)
        u = gi - b
        m_prev = m_s[...]
        big_m = jnp.maximum(_cummax_rows(u), m_prev)
        a_all = jnp.exp(m_prev - big_m)
        e_all = jnp.exp(-(b + big_m))
        m_end = big_m[ch - 1:ch, :]
        ws_all = jnp.exp(u - m_end)
        ae_all = a_all[ch - 1:ch, :]
        gt = gt_ref[j]
        r1, r2, r3 = _split3(jnp.where(trow >= N_HEADS, gt, 0.0))
        b_t = _dot(r1, tri_t) + _dot(r2, tri_t) + _dot(r3, tri_t)
        u_t = gt[0:N_HEADS, :] - b_t[N_HEADS:2 * N_HEADS, :]
        for h in range(N_HEADS):
            w = jnp.exp(jnp.where(tril, u_t[h:h + 1, :] - big_m[:, h:h + 1], NEG_INF))
            q = qkv_ref[r0:r0 + ch, h * DK_A:(h + 1) * DK_A]
            k = qkv_ref[r0:r0 + ch, QK_W + h * DK_A:QK_W + (h + 1) * DK_A]
            v = qkv_ref[r0:r0 + ch, 2 * QK_W + h * DV_A:2 * QK_W + (h + 1) * DV_A]
            sw = _dot_nt(q, k) * w
            qc = _dot(q, caug_s[h].astype(BF16))
            a = a_all[:, h:h + 1]
            num = _dot(sw.astype(BF16), v) + a * qc[:, :DV_A]
            den = jnp.sum(sw, axis=1, keepdims=True) + a * qc[:, DV_A:DV_A + 1]
            rden = 1.0 / jnp.maximum(jnp.abs(den), e_all[:, h:h + 1])
            og = o_ref[r0:r0 + ch, h * DV_A:(h + 1) * DV_A]
            hh = num * rden * (1.0 / (1.0 + jnp.exp(-og)))
            h_s[r0:r0 + ch, h * DV_A:(h + 1) * DV_A] = hh.astype(BF16)
            kw = (ws_all[:, h:h + 1] * k.astype(F32)).astype(BF16)
            vaug = jnp.concatenate([v, ones_blk], axis=1)
            caug_s[h] = ae_all[:, h:h + 1] * caug_s[h] + _dot_tn(kw, vaug)
        m_s[...] = b[ch - 1:ch, :] + m_end

    y = _dot(h_s[...], wout_ref[...])
    y_ref[...] = _layer_norm(ALPHA * x_ref[...] + y, lng_ref[...], lnb_ref[...])

    @pl.when(t == pl.num_programs(1) - 1)
    def _():
        ct_ref[0] = caug_s[...]
        mt_ref[0] = m_s[...]


def _mlstm(qkv, o, gi, gf, gt, x, c0, m0, wout, lng, lnb, *, batch, seq, tc, ch):
    n_t = seq // tc
    rows = batch * seq
    kern = functools.partial(_mlstm_kernel, tc=tc, ch=ch)
    rmap = lambda b, t: (b * n_t + t, 0)
    cmap = lambda b, t: (0, 0)
    smap = lambda b, t: (b, 0, 0, 0)
    mmap = lambda b, t: (b, 0, 0)
    return pl.pallas_call(
        kern,
        grid=(batch, n_t),
        in_specs=[pl.BlockSpec((tc, qkv.shape[1]), rmap),
                  pl.BlockSpec((tc, V_W), rmap),
                  pl.BlockSpec((tc, LANES), rmap),
                  pl.BlockSpec((tc, LANES), rmap),
                  pl.BlockSpec((tc // ch, 2 * N_HEADS, ch), lambda b, t: (b * n_t + t, 0, 0)),
                  pl.BlockSpec((tc, D_MODEL), rmap),
                  pl.BlockSpec((1, N_HEADS, DK_A, 2 * DV_A), smap),
                  pl.BlockSpec((1, 1, LANES), mmap),
                  pl.BlockSpec(wout.shape, cmap),
                  pl.BlockSpec(lng.shape, cmap),
                  pl.BlockSpec(lnb.shape, cmap)],
        out_specs=[pl.BlockSpec((tc, D_MODEL), rmap),
                   pl.BlockSpec((1, N_HEADS, DK_A, 2 * DV_A), smap),
                   pl.BlockSpec((1, 1, LANES), mmap)],
        out_shape=[jax.ShapeDtypeStruct((rows, D_MODEL), F32),
                   jax.ShapeDtypeStruct((batch, N_HEADS, DK_A, 2 * DV_A), F32),
                   jax.ShapeDtypeStruct((batch, 1, LANES), F32)],
        scratch_shapes=[pltpu.VMEM((N_HEADS, DK_A, 2 * DV_A), F32),
                        pltpu.VMEM((1, LANES), F32),
                        pltpu.VMEM((tc, V_W), BF16)],
        compiler_params=_cparams(2),
        name="mlstm_scan",
    )(qkv, o, gi, gf, gt, x, c0, m0, wout, lng, lnb)


def _mlp_kernel(x_ref, wup_ref, wdn_ref, lng_ref, lnb_ref, y_ref, acc_s, xb_s):
    j = pl.program_id(1)

    @pl.when(j == 0)
    def _():
        xb_s[...] = x_ref[...].astype(BF16)
        acc_s[...] = jnp.zeros_like(acc_s)

    hid = jnp.maximum(_dot(xb_s[...], wup_ref[...]), 0.0)
    acc_s[...] += _dot((hid * hid).astype(BF16), wdn_ref[...])

    @pl.when(j == pl.num_programs(1) - 1)
    def _():
        y_ref[...] = _layer_norm(ALPHA * x_ref[...] + acc_s[...], lng_ref[...], lnb_ref[...])


def _mlp(x, wup, wdn, lng, lnb, tm, tf, name="mlp_ln"):
    rows = x.shape[0]
    return pl.pallas_call(
        _mlp_kernel,
        grid=(rows // tm, D_FF // tf),
        in_specs=[pl.BlockSpec((tm, D_MODEL), lambda i, j: (i, 0)),
                  pl.BlockSpec((D_MODEL, tf), lambda i, j: (0, j)),
                  pl.BlockSpec((tf, D_MODEL), lambda i, j: (j, 0)),
                  pl.BlockSpec(lng.shape, lambda i, j: (0, 0)),
                  pl.BlockSpec(lnb.shape, lambda i, j: (0, 0))],
        out_specs=pl.BlockSpec((tm, D_MODEL), lambda i, j: (i, 0)),
        out_shape=jax.ShapeDtypeStruct((rows, D_MODEL), F32),
        scratch_shapes=[pltpu.VMEM((tm, D_MODEL), F32), pltpu.VMEM((tm, D_MODEL), BF16)],
        compiler_params=_cparams(2),
        name=name,
    )(x, wup, wdn, lng, lnb)


def _qkv_b_kernel(x_ref, w_ref, q_ref, k_ref, v_ref, kt_ref, vb_ref, mean_ref, *, tm):
    xh = x_ref[...].astype(BF16)
    step = 512
    for c0 in range(0, D_MODEL, step):
        q_ref[:, c0:c0 + step] = _dot(xh, w_ref[:, c0:c0 + step]).astype(BF16)
        kk = _dot(xh, w_ref[:, D_MODEL + c0:D_MODEL + c0 + step])
        k_ref[:, c0:c0 + step] = kk
        for hh in range(step // HD_B):
            kt_ref[c0 // HD_B + hh] = kk[:, hh * HD_B:(hh + 1) * HD_B].T.astype(BF16)
        for r in range(tm // BLOCK):
            mean_ref[r, :, c0:c0 + step] = jnp.mean(kk[r * BLOCK:(r + 1) * BLOCK], axis=0, keepdims=True)
        vv = _dot(xh, w_ref[:, 2 * D_MODEL + c0:2 * D_MODEL + c0 + step])
        v_ref[:, c0:c0 + step] = vv
        vb_ref[:, c0:c0 + step] = vv.astype(BF16)


def _qkv_b(x, w, tm):
    rows = x.shape[0]
    rmap = lambda i: (i, 0)
    rspec = pl.BlockSpec((tm, D_MODEL), rmap)
    return pl.pallas_call(
        functools.partial(_qkv_b_kernel, tm=tm),
        grid=(rows // tm,),
        in_specs=[rspec, pl.BlockSpec(w.shape, lambda i: (0, 0))],
        out_specs=[rspec, rspec, rspec,
                   pl.BlockSpec((N_HEADS, HD_B, tm), lambda i: (0, 0, i)),
                   rspec,
                   pl.BlockSpec((tm // BLOCK, 1, D_MODEL), lambda i: (i, 0, 0))],
        out_shape=[jax.ShapeDtypeStruct((rows, D_MODEL), BF16),
                   jax.ShapeDtypeStruct((rows, D_MODEL), F32),
                   jax.ShapeDtypeStruct((rows, D_MODEL), F32),
                   jax.ShapeDtypeStruct((N_HEADS, HD_B, rows), BF16),
                   jax.ShapeDtypeStruct((rows, D_MODEL), BF16),
                   jax.ShapeDtypeStruct((rows // BLOCK, 1, D_MODEL), F32)],
        compiler_params=_cparams(1),
        name="moba_qkv_proj",
    )(x, w)


def _out_ln_kernel(a_ref, w_ref, x_ref, lng_ref, lnb_ref, y_ref):
    y = _dot(a_ref[...].astype(BF16), w_ref[...])
    y_ref[...] = _layer_norm(ALPHA * x_ref[...] + y, lng_ref[...], lnb_ref[...])


def _out_ln(a, w, x, lng, lnb, tm):
    rows = x.shape[0]
    rspec = pl.BlockSpec((tm, D_MODEL), lambda i: (i, 0))
    cmap = lambda i: (0, 0)
    return pl.pallas_call(
        _out_ln_kernel,
        grid=(rows // tm,),
        in_specs=[rspec, pl.BlockSpec(w.shape, cmap), rspec,
                  pl.BlockSpec(lng.shape, cmap), pl.BlockSpec(lnb.shape, cmap)],
        out_specs=rspec,
        out_shape=jax.ShapeDtypeStruct((rows, D_MODEL), F32),
        compiler_params=_cparams(1),
        name="attn_out_ln",
    )(a, w, x, lng, lnb)


def _head_slopes(rep):
    slopes = np.zeros((1, LANES), np.float32)
    slopes[0, :N_HEADS * rep] = np.repeat(2.0 ** -np.arange(1, N_HEADS + 1, dtype=np.float64), rep)
    return jnp.asarray(slopes)


def _head_block_diag(rows_8):
    tiled = jnp.concatenate([rows_8] * (LANES // 8), axis=0)
    r = lax.broadcasted_iota(jnp.int32, (LANES, D_MODEL), 0)
    c = lax.broadcasted_iota(jnp.int32, (LANES, D_MODEL), 1)
    keep = (r // 8) == (c // HD_B)
    return jnp.where(keep, tiled, 0.0).astype(BF16)


def _moba_prompt_kernel(qt_ref, kn_ref, q_ref, kt_ref, vb_ref, mean_ref, slope_ref, o_ref,
                        sel_s, mask_s, m_s, l_s, acc_s):
    p = pl.program_id(1)
    qt = qt_ref[p]
    kn = kn_ref[p]
    scale = HD_B ** -0.5
    rows = Q_TILE
    nqb = Q_TILE // BLOCK
    row_blk = lax.broadcasted_iota(jnp.int32, (rows, LANES), 0) // BLOCK + nqb * qt

    @pl.when(jnp.logical_and(pl.program_id(0) == 0, p == 0))
    def _():
        t_i = lax.broadcasted_iota(jnp.int32, (BLOCK, BLOCK), 0)
        j_i = lax.broadcasted_iota(jnp.int32, (BLOCK, BLOCK), 1)
        jf = j_i.astype(F32)
        for h in range(N_HEADS):
            mask_s[0, h] = (2.0 ** -(h + 1)) * jf
            mask_s[1, h] = jnp.where(j_i > t_i, NEG_INF, (2.0 ** -(h + 1)) * jf)

    @pl.when(kn == 0)
    def _():
        mbd = _head_block_diag(mean_ref[0])
        sc = _dot_nt(q_ref[...], mbd)
        lane = lax.broadcasted_iota(jnp.int32, sc.shape, 1)
        blk = lane % 8
        past = jnp.logical_and(lane < N_HEADS * 8, blk < row_blk)
        sm = jnp.where(past, sc, NEG_INF)
        rank = jnp.zeros(sc.shape, F32)
        for r in range(1, 8):
            lo = pltpu.roll(sm, r, 1)
            rank += jnp.where(blk >= r, jnp.where(lo >= sm, 1.0, 0.0), 0.0)
            hi = pltpu.roll(sm, LANES - r, 1)
            rank += jnp.where(blk + r < 8, jnp.where(hi > sm, 1.0, 0.0), 0.0)
        sel_s[...] = jnp.where(past, jnp.where(rank < TOPK, 1.0, 0.0), 0.0)
        m_s[...] = jnp.full(m_s.shape, M_FLOOR, F32)
        l_s[...] = jnp.zeros_like(l_s)
        acc_s[...] = jnp.zeros_like(acc_s)

    er = lax.broadcasted_iota(jnp.int32, (LANES, LANES), 0)
    ec = lax.broadcasted_iota(jnp.int32, (LANES, LANES), 1)
    pick = jnp.where(jnp.logical_and(er == ec * 8 + kn, ec < N_HEADS), 1.0, 0.0).astype(BF16)
    selcols = _dot(sel_s[...].astype(BF16), pick)
    kbase = slope_ref[...] * (kn * BLOCK).astype(F32)
    attend = jnp.logical_or(selcols > 0.5, row_blk == kn)
    biasc = jnp.where(attend, kbase, NEG_INF)
    ones_v = jnp.ones((BLOCK, HD_B), BF16)
    own_i = [(kn == nqb * qt + j).astype(jnp.int32) for j in range(nqb)]

    for h in range(N_HEADS):
        cs = slice(h * HD_B, (h + 1) * HD_B)
        vaug = jnp.concatenate([vb_ref[:, cs], ones_v], axis=1)
        bias = jnp.broadcast_to(biasc[:, h:h + 1], (rows, LANES))
        mask = jnp.concatenate([mask_s[own_i[j], h] for j in range(nqb)], axis=0)
        s = _dot(q_ref[:, cs], kt_ref[h])
        lg = s * scale + mask + jnp.concatenate([bias, bias], axis=1)
        m_old = m_s[h]
        m_new = jnp.maximum(m_old, jnp.max(lg, axis=1, keepdims=True))
        alpha = jnp.exp(m_old - m_new)
        pr = jnp.exp(lg - jnp.concatenate([m_new, m_new], axis=1))
        pv = _dot(pr.astype(BF16), vaug)
        l_s[h] = alpha * l_s[h] + pv[:, HD_B:]
        acc_s[:, cs] = alpha * acc_s[:, cs] + pv[:, :HD_B]
        m_s[h] = m_new

    @pl.when(kn == nqb * qt + nqb - 1)
    def _():
        for h in range(N_HEADS):
            cs = slice(h * HD_B, (h + 1) * HD_B)
            o_ref[:, cs] = (acc_s[:, cs] / l_s[h]).astype(BF16)


def _moba_prompt(q, kt, vb, means, *, batch, seq):
    nb = seq // BLOCK
    n_qt = seq // Q_TILE
    nqb = Q_TILE // BLOCK
    qt_l, kn_l = [], []
    for i in range(n_qt):
        qt_l += [i] * (nqb * i + nqb)
        kn_l += list(range(nqb * i + nqb))
    qt_arr = jnp.asarray(np.array(qt_l, np.int32))
    kn_arr = jnp.asarray(np.array(kn_l, np.int32))
    grid_spec = pltpu.PrefetchScalarGridSpec(
        num_scalar_prefetch=2,
        grid=(batch, len(qt_l)),
        in_specs=[pl.BlockSpec((Q_TILE, D_MODEL), lambda b, p, qt, kn: (b * n_qt + qt[p], 0)),
                  pl.BlockSpec((N_HEADS, HD_B, BLOCK), lambda b, p, qt, kn: (0, 0, b * nb + kn[p])),
                  pl.BlockSpec((BLOCK, D_MODEL), lambda b, p, qt, kn: (b * nb + kn[p], 0)),
                  pl.BlockSpec((1, nb, D_MODEL), lambda b, p, qt, kn: (b, 0, 0)),
                  pl.BlockSpec((1, LANES), lambda b, p, qt, kn: (0, 0))],
        out_specs=pl.BlockSpec((Q_TILE, D_MODEL), lambda b, p, qt, kn: (b * n_qt + qt[p], 0)),
        scratch_shapes=[pltpu.VMEM((Q_TILE, LANES), F32),
                        pltpu.VMEM((2, N_HEADS, BLOCK, BLOCK), F32),
                        pltpu.VMEM((N_HEADS, Q_TILE, LANES), F32),
                        pltpu.VMEM((N_HEADS, Q_TILE, LANES), F32),
                        pltpu.VMEM((Q_TILE, D_MODEL), F32)])
    return pl.pallas_call(
        _moba_prompt_kernel,
        grid_spec=grid_spec,
        out_shape=jax.ShapeDtypeStruct((batch * seq, D_MODEL), BF16),
        compiler_params=_cparams(2),
        name="moba_prompt_attn",
    )(qt_arr, kn_arr, q, kt, vb, means, _head_slopes(1))


PAGES_PER_STEP = 8


def _moba_sample_kernel(pt_ref, q_ref, kn_ref, vn_ref, *refs, n_pages, dec_seq):
    kp_refs = refs[:PAGES_PER_STEP]
    vp_refs = refs[PAGES_PER_STEP:2 * PAGES_PER_STEP]
    o_ref = refs[2 * PAGES_PER_STEP]
    lt_s, po_s, rden_s, acc_s = refs[2 * PAGES_PER_STEP + 1:]
    ph = pl.program_id(1)
    s = pl.program_id(2)
    n_steps = pl.num_programs(2)
    past_len = n_pages * PAGE
    n_blocks = past_len // BLOCK
    ppb = BLOCK // PAGE
    scale = HD_B ** -0.5

    def head_rows(page_ref, h):
        return page_ref[0, pl.ds(h, PAGE, stride=N_HEADS), :]

    def pad_page(new_ref, h):
        blk = new_ref[:, h * HD_B:(h + 1) * HD_B]
        return jnp.concatenate([blk, jnp.zeros((PAGE - dec_seq, HD_B), F32)], axis=0).astype(BF16)

    def q_head(h):
        return q_ref[:, h * HD_B:(h + 1) * HD_B].astype(BF16)

    @pl.when(ph == 0)
    def _():
        for i in range(PAGES_PER_STEP):
            pg = s * PAGES_PER_STEP + i
            for h in range(N_HEADS):
                k_t = head_rows(kp_refs[i], h).T.astype(BF16)
                lt_s[h, pg] = _dot(q_head(h), k_t)

    @pl.when(jnp.logical_and(ph == 0, s == n_steps - 1))
    def _():
        lane = lax.broadcasted_iota(jnp.int32, (dec_seq, LANES), 1)
        rows = []
        for h in range(N_HEADS):
            blk_sum = jnp.sum(lt_s[h].reshape(n_blocks, ppb, dec_seq, PAGE), axis=1)
            tot = jnp.sum(blk_sum, axis=2, keepdims=True) * (1.0 / BLOCK)
            sc_h = jnp.full((dec_seq, LANES), NEG_INF, F32)
            for n in range(n_blocks):
                sc_h = jnp.where(lane == n, tot[n], sc_h)
            rows.append(sc_h)
        sc = jnp.concatenate(rows, axis=0)
        lane_a = lax.broadcasted_iota(jnp.int32, sc.shape, 1)
        rank = jnp.zeros(sc.shape, F32)
        for r in range(1, n_blocks):
            lo = pltpu.roll(sc, r, 1)
            rank += jnp.where(lane_a >= r, jnp.where(lo >= sc, 1.0, 0.0), 0.0)
            hi = pltpu.roll(sc, LANES - r, 1)
            rank += jnp.where(lane_a + r < n_blocks, jnp.where(hi > sc, 1.0, 0.0), 0.0)
        sel_bias = jnp.where(jnp.logical_and(rank < TOPK, lane_a < n_blocks), 0.0, NEG_INF)
        kpos = (lax.broadcasted_iota(jnp.int32, (n_pages, 1, PAGE), 0) * PAGE
                + lax.broadcasted_iota(jnp.int32, (n_pages, 1, PAGE), 2) - past_len).astype(F32)
        qi = lax.broadcasted_iota(jnp.int32, (dec_seq, PAGE), 0)
        ki = lax.broadcasted_iota(jnp.int32, (dec_seq, PAGE), 1)
        for h in range(N_HEADS):
            slope = 2.0 ** -(h + 1)
            bias_h = sel_bias[h * dec_seq:(h + 1) * dec_seq, :]
            bias_pages = jnp.stack([jnp.broadcast_to(bias_h[:, n:n + 1], (dec_seq, PAGE))
                                    for n in range(n_blocks) for _ in range(ppb)], axis=0)
            lgs = lt_s[h] * scale + slope * kpos + bias_pages
            lgo = _dot_nt(q_head(h), pad_page(kn_ref, h))
            lgo = jnp.where(ki <= qi, lgo * scale + slope * ki.astype(F32), NEG_INF)
            mx = jnp.maximum(jnp.max(jnp.max(lgs, axis=0), axis=1, keepdims=True),
                             jnp.max(lgo, axis=1, keepdims=True))
            pr = jnp.exp(lgs - mx[None])
            po = jnp.exp(lgo - mx)
            den = jnp.sum(jnp.sum(pr, axis=0), axis=1, keepdims=True) + jnp.sum(po, axis=1, keepdims=True)
            lt_s[h] = pr
            po_s[h] = po
            rden_s[h] = jnp.broadcast_to(1.0 / den, (dec_seq, LANES))

    @pl.when(jnp.logical_and(ph == 1, s == 0))
    def _():
        for h in range(N_HEADS):
            acc_s[:, h * HD_B:(h + 1) * HD_B] = _dot(po_s[h].astype(BF16), pad_page(vn_ref, h))

    @pl.when(ph == 1)
    def _():
        for i in range(PAGES_PER_STEP):
            pg = s * PAGES_PER_STEP + i
            for h in range(N_HEADS):
                acc_s[:, h * HD_B:(h + 1) * HD_B] += _dot(lt_s[h, pg].astype(BF16),
                                                          head_rows(vp_refs[i], h).astype(BF16))

    @pl.when(jnp.logical_and(ph == 1, s == n_steps - 1))
    def _():
        for h in range(N_HEADS):
            cs = slice(h * HD_B, (h + 1) * HD_B)
            o_ref[:, cs] = acc_s[:, cs] * rden_s[h]


def _moba_sample(page_table, q, k_new, v_new, cache_k, cache_v, *, batch, dec_seq):
    n_pages = page_table.shape[1]
    n_steps = n_pages // PAGES_PER_STEP
    rspec = pl.BlockSpec((dec_seq, D_MODEL), lambda b, ph, s, pt: (b, 0))

    def kmap(i):
        return lambda b, ph, s, pt: (pt[b, jnp.where(ph == 0, s, n_steps - 1) * PAGES_PER_STEP + i], 0, 0)

    def vmap_(i):
        return lambda b, ph, s, pt: (pt[b, jnp.where(ph == 0, 0, s) * PAGES_PER_STEP + i], 0, 0)

    pspec = lambda m: pl.BlockSpec((1, PAGE * N_HEADS, HD_B), m)
    grid_spec = pltpu.PrefetchScalarGridSpec(
        num_scalar_prefetch=1,
        grid=(batch, 2, n_steps),
        in_specs=[rspec, rspec, rspec]
                 + [pspec(kmap(i)) for i in range(PAGES_PER_STEP)]
                 + [pspec(vmap_(i)) for i in range(PAGES_PER_STEP)],
        out_specs=rspec,
        scratch_shapes=[pltpu.VMEM((N_HEADS, n_pages, dec_seq, PAGE), F32),
                        pltpu.VMEM((N_HEADS, dec_seq, PAGE), F32),
                        pltpu.VMEM((N_HEADS, dec_seq, LANES), F32),
                        pltpu.VMEM((dec_seq, D_MODEL), F32)])
    kern = functools.partial(_moba_sample_kernel, n_pages=n_pages, dec_seq=dec_seq)
    return pl.pallas_call(
        kern,
        grid_spec=grid_spec,
        out_shape=jax.ShapeDtypeStruct((batch * dec_seq, D_MODEL), F32),
        compiler_params=_cparams(3),
        name="moba_sample_attn",
    )(page_table, q, k_new, v_new, *([cache_k] * PAGES_PER_STEP), *([cache_v] * PAGES_PER_STEP))


def kernel(x_prompt, x_sample, state_C, state_n, state_m, cache_k, cache_v, page_table,
           w_in_a, b_gate_a, w_out_a, w_q_b, w_out_b, w_kv, w_up, w_down, ln_g, ln_b):
    bp, sp, _ = x_prompt.shape
    bs, ss, _ = x_sample.shape
    ch = MLSTM_CHUNK
    nq = 2 * QK_W + 2 * V_W

    wm_a = w_in_a[0, :, :nq].astype(BF16)
    wg_a = jnp.pad(w_in_a[0, :, nq:], ((0, 0), (0, LANES - 2 * N_HEADS)))
    bg_a = jnp.pad(b_gate_a[0], (0, LANES - 2 * N_HEADS)).reshape(1, LANES)
    wout_a = w_out_a[0].astype(BF16)
    w_qkv_b = jnp.concatenate([w_q_b[0], w_kv], axis=1).astype(BF16)
    wout_b = w_out_b[0].astype(BF16)
    wup = w_up.astype(BF16)
    wdn = w_down.astype(BF16)
    lng = ln_g.reshape(DEPTH, 2, 1, D_MODEL)
    lnb = ln_b.reshape(DEPTH, 2, 1, D_MODEL)

    xp = x_prompt.reshape(bp * sp, D_MODEL)
    qkv, o, gi, gf, gt = _proj_a(xp, wm_a, wg_a, bg_a, 512, ch)
    c0 = jnp.zeros((bp, N_HEADS, DK_A, 2 * DV_A), F32)
    m0 = jnp.zeros((bp, 1, LANES), F32)
    x1, caug_p, m_p = _mlstm(qkv, o, gi, gf, gt, xp, c0, m0, wout_a, lng[0, 0], lnb[0, 0],
                             batch=bp, seq=sp, tc=512, ch=ch)
    x2 = _mlp(x1, wup[0], wdn[0], lng[0, 1], lnb[0, 1], 1024, 1024)
    q, k, v, kt, vb, means = _qkv_b(x2, w_qkv_b, 512)
    att = _moba_prompt(q, kt, vb, means.reshape(bp, sp // BLOCK, D_MODEL), batch=bp, seq=sp)
    x3 = _out_ln(att, wout_b, x2, lng[1, 0], lnb[1, 0], 512)
    y_prompt = _mlp(x3, wup[1], wdn[1], lng[1, 1], lnb[1, 1], 1024, 1024)

    rows_s = bs * ss
    xs = x_sample.reshape(rows_s, D_MODEL)
    qkv_s, o_s, gi_s, gf_s, _ = _proj_a(xs, wm_a, wg_a, bg_a, rows_s, ch)
    pad3 = lambda a, val=0.0: jnp.pad(a.reshape(bs, ss, -1), ((0, 0), (0, ch - ss), (0, 0)),
                                      constant_values=val)
    flat = lambda a: a.reshape(bs * ch, -1)
    gi_pad = jnp.where(jnp.arange(LANES)[None, None, :] < N_HEADS, pad3(gi_s, NEG_INF), 0.0)
    gf_pad = pad3(gf_s)
    gt_pad = jnp.concatenate([jnp.swapaxes(gi_pad, 1, 2)[:, :N_HEADS],
                              jnp.swapaxes(gf_pad, 1, 2)[:, :N_HEADS]], axis=1)
    c0s = jnp.concatenate([state_C[0], state_n[0][..., None],
                           jnp.zeros((bs, N_HEADS, DK_A, DV_A - 1), F32)], axis=-1)
    m0s = jnp.pad(state_m[0], ((0, 0), (0, LANES - N_HEADS))).reshape(bs, 1, LANES)
    x1s_pad, caug_s, m_s = _mlstm(flat(pad3(qkv_s)), flat(pad3(o_s)), flat(gi_pad), flat(gf_pad), gt_pad,
                                  flat(pad3(xs)), c0s, m0s, wout_a, lng[0, 0], lnb[0, 0],
                                  batch=bs, seq=ch, tc=ch, ch=ch)
    x1s = x1s_pad.reshape(bs, ch, D_MODEL)[:, :ss].reshape(rows_s, D_MODEL)
    x2s = _mlp(x1s, wup[0], wdn[0], lng[0, 1], lnb[0, 1], rows_s, 512)
    q_s, k_s, v_s, _, _, _ = _qkv_b(x2s, w_qkv_b, rows_s)
    n_pool = cache_k.shape[0]
    att_s = _moba_sample(page_table, q_s.astype(F32), k_s, v_s,
                         cache_k.reshape(n_pool, PAGE * N_HEADS, HD_B),
                         cache_v.reshape(n_pool, PAGE * N_HEADS, HD_B), batch=bs, dec_seq=ss)
    x3s = _out_ln(att_s, wout_b, x2s, lng[1, 0], lnb[1, 0], rows_s)
    y_sample = _mlp(x3s, wup[1], wdn[1], lng[1, 1], lnb[1, 1], rows_s, 512)

    def split_state(caug, m):
        return (caug[None, :, :, :, :DV_A], caug[None, :, :, :, DV_A], m[None, :, 0, :N_HEADS])

    p_c, p_n, p_m = split_state(caug_p, m_p)
    s_c, s_n, s_m = split_state(caug_s, m_s)
    return (y_prompt.reshape(bp, sp, D_MODEL), y_sample.reshape(bs, ss, D_MODEL),
            p_c, p_n, p_m,
            k.reshape(bp, sp, N_HEADS, HD_B), v.reshape(bp, sp, N_HEADS, HD_B),
            s_c, s_n, s_m,
            k_s.reshape(bs, ss, N_HEADS, HD_B), v_s.reshape(bs, ss, N_HEADS, HD_B))
```

```python
import functools

import numpy as np
import jax
import jax.numpy as jnp
from jax import lax
from jax.experimental import pallas as pl
from jax.experimental.pallas import tpu as pltpu

F32 = jnp.float32
BF16 = jnp.bfloat16

D_MODEL = 1024
N_HEADS = 8
DK_A = 64
DV_A = 128
QK_W = N_HEADS * DK_A
V_W = N_HEADS * DV_A
HD_B = 128
BLOCK = 256
TOPK = 3
PAGE = 128
D_FF = 4 * D_MODEL
DEPTH = 2
ALPHA = (2.0 * DEPTH) ** 0.25
LN_EPS = 1e-5
NEG_INF = float("-inf")

LANES = 128
VMEM_LIMIT = 56 * 1024 * 1024

MLSTM_CHUNK = 128
Q_TILE = BLOCK
M_FLOOR = -1e30


def _cparams(n_axes, flags=None):
    return pltpu.CompilerParams(dimension_semantics=("arbitrary",) * n_axes,
                                vmem_limit_bytes=VMEM_LIMIT, flags=flags)


def _dot(a, b):
    return jnp.dot(a, b, preferred_element_type=F32)


def _dot_nt(a, b):
    return lax.dot_general(a, b, (((1,), (1,)), ((), ())), preferred_element_type=F32)


def _dot_tn(a, b):
    return lax.dot_general(a, b, (((0,), (0,)), ((), ())), preferred_element_type=F32)


def _split3(x):
    x1 = x.astype(BF16)
    r1 = x - x1.astype(F32)
    x2 = r1.astype(BF16)
    x3 = (r1 - x2.astype(F32)).astype(BF16)
    return x1, x2, x3


def _layer_norm(y, g, b):
    mu = jnp.mean(y, axis=-1, keepdims=True)
    d = y - mu
    var = jnp.mean(d * d, axis=-1, keepdims=True)
    return d * lax.rsqrt(var + LN_EPS) * g + b


def _cummax_rows(x):
    row = lax.broadcasted_iota(jnp.int32, x.shape, 0)
    shift = 1
    while shift < x.shape[0]:
        x = jnp.maximum(x, jnp.where(row >= shift, pltpu.roll(x, shift, 0), NEG_INF))
        shift *= 2
    return x


def _proj_a_kernel(x_ref, wm_ref, wg_ref, bg_ref, qv_ref, kt_ref, o_ref, gi_ref, gf_ref, gt_ref, *, ch):
    x = x_ref[...]
    xh = x.astype(BF16)
    nq = 2 * QK_W + V_W
    step = 512
    for c0 in range(0, nq, step):
        acc = _dot(xh, wm_ref[:, c0:c0 + step])
        if c0 == QK_W:
            kt_ref[...] = (acc * (DK_A ** -0.5)).T.astype(BF16)
        else:
            dst = c0 if c0 < QK_W else c0 - QK_W
            qv_ref[:, dst:dst + step] = acc.astype(BF16)
    for c0 in range(0, V_W, step):
        o_ref[:, c0:c0 + step] = _dot(xh, wm_ref[:, nq + c0:nq + c0 + step])
    xl = (x - xh.astype(F32)).astype(BF16)
    wg = wg_ref[...]
    wgh = wg.astype(BF16)
    wgl = (wg - wgh.astype(F32)).astype(BF16)
    g = _dot(xh, wgh) + _dot(xl, wgh) + _dot(xh, wgl) + bg_ref[...]
    lane = lax.broadcasted_iota(jnp.int32, g.shape, 1)
    log_f = jnp.minimum(g, 0.0) - jnp.log1p(jnp.exp(-jnp.abs(g)))
    gi_ref[...] = jnp.where(lane < N_HEADS, g, 0.0)
    gf_ref[...] = jnp.where(lane < N_HEADS, pltpu.roll(log_f, LANES - N_HEADS, 1), 0.0)
    both_t = jnp.where(lane < N_HEADS, g, jnp.where(lane < 2 * N_HEADS, log_f, 0.0)).T
    for c in range(x.shape[0] // ch):
        gt_ref[c] = both_t[0:2 * N_HEADS, c * ch:(c + 1) * ch]


def _proj_a(x, wm, wg, bg, tm, ch):
    rows = x.shape[0]
    return pl.pallas_call(
        functools.partial(_proj_a_kernel, ch=ch),
        grid=(rows // tm,),
        in_specs=[pl.BlockSpec((tm, D_MODEL), lambda i: (i, 0)),
                  pl.BlockSpec(wm.shape, lambda i: (0, 0)),
                  pl.BlockSpec(wg.shape, lambda i: (0, 0)),
                  pl.BlockSpec(bg.shape, lambda i: (0, 0))],
        out_specs=[pl.BlockSpec((tm, QK_W + V_W), lambda i: (i, 0)),
                   pl.BlockSpec((QK_W, tm), lambda i: (0, i)),
                   pl.BlockSpec((tm, V_W), lambda i: (i, 0)),
                   pl.BlockSpec((tm, LANES), lambda i: (i, 0)),
                   pl.BlockSpec((tm, LANES), lambda i: (i, 0)),
                   pl.BlockSpec((tm // ch, 2 * N_HEADS, ch), lambda i: (i, 0, 0))],
        out_shape=[jax.ShapeDtypeStruct((rows, QK_W + V_W), BF16),
                   jax.ShapeDtypeStruct((QK_W, rows), BF16),
                   jax.ShapeDtypeStruct((rows, V_W), F32),
                   jax.ShapeDtypeStruct((rows, LANES), F32),
                   jax.ShapeDtypeStruct((rows, LANES), F32),
                   jax.ShapeDtypeStruct((rows // ch, 2 * N_HEADS, ch), F32)],
        compiler_params=_cparams(1),
        name="mlstm_in_proj",
    )(x, wm, wg, bg)


def _mlstm_kernel(qv_ref, kt_ref, o_ref, gi_ref, gf_ref, gt_ref, x_ref, c0_ref, n0_ref, m0_ref, wout_ref,
                  lng_ref, lnb_ref, y_ref, ct_ref, nt_ref, mt_ref, caug_s, m_s, h_s, *, tc, ch):
    t = pl.program_id(1)
    eye_k = (lax.broadcasted_iota(jnp.int32, (DK_A, DK_A), 0)
             == lax.broadcasted_iota(jnp.int32, (DK_A, DK_A), 1))

    @pl.when(t == 0)
    def _():
        for h in range(N_HEADS):
            caug_s[h, :, :DV_A] = c0_ref[0, h]
            n_row = jnp.broadcast_to(n0_ref[0, h:h + 1, :], (DK_A, DK_A))
            n_col = jnp.sum(jnp.where(eye_k, n_row, 0.0), axis=1, keepdims=True)
            caug_s[h, :, DV_A:] = jnp.broadcast_to(n_col, (DK_A, DV_A))
        m_s[...] = m0_ref[0]

    row = lax.broadcasted_iota(jnp.int32, (ch, ch), 0)
    col = lax.broadcasted_iota(jnp.int32, (ch, ch), 1)
    tril = row >= col
    tri_b = jnp.where(tril, 1.0, 0.0).astype(BF16)
    tri_t = jnp.where(row <= col, 1.0, 0.0).astype(BF16)
    ones_blk = jnp.ones((ch, LANES), BF16)
    trow = lax.broadcasted_iota(jnp.int32, (2 * N_HEADS, ch), 0)

    for j in range(tc // ch):
        r0 = j * ch
        gi = gi_ref[r0:r0 + ch, :]
        f1, f2, f3 = _split3(gf_ref[r0:r0 + ch, :])
        b = _dot(tri_b, f1) + _dot(tri_b, f2) + _dot(tri_b, f3)
        u = gi - b
        m_prev = m_s[...]
        big_m = jnp.maximum(_cummax_rows(u), m_prev)
        a_all = jnp.exp(m_prev - big_m)
        e_all = jnp.exp(-(b + big_m))
        m_end = big_m[ch - 1:ch, :]
        ae_all = a_all[ch - 1:ch, :]
        gt = gt_ref[j]
        r1, r2, r3 = _split3(jnp.where(trow >= N_HEADS, gt, 0.0))
        b_t = _dot(r1, tri_t) + _dot(r2, tri_t) + _dot(r3, tri_t)
        u_t = gt[0:N_HEADS, :] - b_t[N_HEADS:2 * N_HEADS, :]
        for h in range(N_HEADS):
            w = jnp.exp(jnp.where(tril, u_t[h:h + 1, :] - big_m[:, h:h + 1], NEG_INF))
            q = qv_ref[r0:r0 + ch, h * DK_A:(h + 1) * DK_A]
            k_t = kt_ref[h * DK_A:(h + 1) * DK_A, r0:r0 + ch]
            v = qv_ref[r0:r0 + ch, QK_W + h * DV_A:QK_W + (h + 1) * DV_A]
            vaug = jnp.concatenate([v, ones_blk], axis=1)
            sw = _dot(q, k_t) * w
            qc = _dot(q, caug_s[h].astype(BF16))
            a = jnp.broadcast_to(a_all[:, h:h + 1], (ch, LANES))
            tot = _dot(sw.astype(BF16), vaug) + jnp.concatenate([a, a], axis=1) * qc
            floor = jnp.broadcast_to(e_all[:, h:h + 1], (ch, LANES))
            rden = 1.0 / jnp.maximum(jnp.abs(tot[:, DV_A:]), floor)
            og = o_ref[r0:r0 + ch, h * DV_A:(h + 1) * DV_A]
            hh = tot[:, :DV_A] * rden * (1.0 / (1.0 + jnp.exp(-og)))
            h_s[r0:r0 + ch, h * DV_A:(h + 1) * DV_A] = hh.astype(BF16)
            ws_t = jnp.exp(u_t[h:h + 1, :] - m_end[:, h:h + 1])
            kw_t = (k_t.astype(F32) * ws_t).astype(BF16)
            caug_s[h] = ae_all[:, h:h + 1] * caug_s[h] + _dot(kw_t, vaug)
        m_s[...] = b[ch - 1:ch, :] + m_end

    y = _dot(h_s[...], wout_ref[...])
    y_ref[...] = _layer_norm(ALPHA * x_ref[...] + y, lng_ref[...], lnb_ref[...])

    @pl.when(t == pl.num_programs(1) - 1)
    def _():
        for h in range(N_HEADS):
            ct_ref[0, h] = caug_s[h, :, :DV_A]
            n_col = jnp.broadcast_to(caug_s[h, :, DV_A:DV_A + 1], (DK_A, DK_A))
            nt_ref[0, h:h + 1, :] = jnp.sum(jnp.where(eye_k, n_col, 0.0), axis=0, keepdims=True)
        mt_ref[0] = m_s[...]


def _mlstm(qv, kt, o, gi, gf, gt, x, c0, n0, m0, wout, lng, lnb, *, batch, seq, tc, ch):
    n_t = seq // tc
    rows = batch * seq
    kern = functools.partial(_mlstm_kernel, tc=tc, ch=ch)
    rmap = lambda b, t: (b * n_t + t, 0)
    cmap = lambda b, t: (0, 0)
    smap = lambda b, t: (b, 0, 0, 0)
    mmap = lambda b, t: (b, 0, 0)
    return pl.pallas_call(
        kern,
        grid=(batch, n_t),
        in_specs=[pl.BlockSpec((tc, QK_W + V_W), rmap),
                  pl.BlockSpec((QK_W, tc), lambda b, t: (0, b * n_t + t)),
                  pl.BlockSpec((tc, V_W), rmap),
                  pl.BlockSpec((tc, LANES), rmap),
                  pl.BlockSpec((tc, LANES), rmap),
                  pl.BlockSpec((tc // ch, 2 * N_HEADS, ch), lambda b, t: (b * n_t + t, 0, 0)),
                  pl.BlockSpec((tc, D_MODEL), rmap),
                  pl.BlockSpec((1, N_HEADS, DK_A, DV_A), smap),
                  pl.BlockSpec((1, N_HEADS, DK_A), mmap),
                  pl.BlockSpec((1, 1, LANES), mmap),
                  pl.BlockSpec(wout.shape, cmap),
                  pl.BlockSpec(lng.shape, cmap),
                  pl.BlockSpec(lnb.shape, cmap)],
        out_specs=[pl.BlockSpec((tc, D_MODEL), rmap),
                   pl.BlockSpec((1, N_HEADS, DK_A, DV_A), smap),
                   pl.BlockSpec((1, N_HEADS, DK_A), mmap),
                   pl.BlockSpec((1, 1, LANES), mmap)],
        out_shape=[jax.ShapeDtypeStruct((rows, D_MODEL), F32),
                   jax.ShapeDtypeStruct((batch, N_HEADS, DK_A, DV_A), F32),
                   jax.ShapeDtypeStruct((batch, N_HEADS, DK_A), F32),
                   jax.ShapeDtypeStruct((batch, 1, LANES), F32)],
        scratch_shapes=[pltpu.VMEM((N_HEADS, DK_A, 2 * DV_A), F32),
                        pltpu.VMEM((1, LANES), F32),
                        pltpu.VMEM((tc, V_W), BF16)],
        compiler_params=_cparams(2),
        name="mlstm_scan",
    )(qv, kt, o, gi, gf, gt, x, c0, n0, m0, wout, lng, lnb)


def _mlp_kernel(x_ref, wup_ref, wdn_ref, lng_ref, lnb_ref, y_ref, acc_s, xb_s):
    j = pl.program_id(1)

    @pl.when(j == 0)
    def _():
        xb_s[...] = x_ref[...].astype(BF16)
        acc_s[...] = jnp.zeros_like(acc_s)

    hid = jnp.maximum(_dot(xb_s[...], wup_ref[...]), 0.0)
    acc_s[...] += _dot((hid * hid).astype(BF16), wdn_ref[...])

    @pl.when(j == pl.num_programs(1) - 1)
    def _():
        y_ref[...] = _layer_norm(ALPHA * x_ref[...] + acc_s[...], lng_ref[...], lnb_ref[...])


def _mlp(x, wup, wdn, lng, lnb, tm, tf, name="mlp_ln"):
    rows = x.shape[0]
    return pl.pallas_call(
        _mlp_kernel,
        grid=(rows // tm, D_FF // tf),
        in_specs=[pl.BlockSpec((tm, D_MODEL), lambda i, j: (i, 0)),
                  pl.BlockSpec((D_MODEL, tf), lambda i, j: (0, j)),
                  pl.BlockSpec((tf, D_MODEL), lambda i, j: (j, 0)),
                  pl.BlockSpec(lng.shape, lambda i, j: (0, 0)),
                  pl.BlockSpec(lnb.shape, lambda i, j: (0, 0))],
        out_specs=pl.BlockSpec((tm, D_MODEL), lambda i, j: (i, 0)),
        out_shape=jax.ShapeDtypeStruct((rows, D_MODEL), F32),
        scratch_shapes=[pltpu.VMEM((tm, D_MODEL), F32), pltpu.VMEM((tm, D_MODEL), BF16)],
        compiler_params=_cparams(2),
        name=name,
    )(x, wup, wdn, lng, lnb)


def _qkv_b_kernel(x_ref, w_ref, q_ref, k_ref, v_ref, kt_ref, vb_ref, mean_ref, *, tm):
    xh = x_ref[...].astype(BF16)
    step = 512
    for c0 in range(0, D_MODEL, step):
        q_ref[:, c0:c0 + step] = _dot(xh, w_ref[:, c0:c0 + step]).astype(BF16)
        kk = _dot(xh, w_ref[:, D_MODEL + c0:D_MODEL + c0 + step])
        k_ref[:, c0:c0 + step] = kk
        for hh in range(step // HD_B):
            kt_ref[c0 // HD_B + hh] = kk[:, hh * HD_B:(hh + 1) * HD_B].T.astype(BF16)
        for r in range(tm // BLOCK):
            mean_ref[r, :, c0:c0 + step] = jnp.mean(kk[r * BLOCK:(r + 1) * BLOCK], axis=0, keepdims=True)
        vv = _dot(xh, w_ref[:, 2 * D_MODEL + c0:2 * D_MODEL + c0 + step])
        v_ref[:, c0:c0 + step] = vv
        vb_ref[:, c0:c0 + step] = vv.astype(BF16)


def _qkv_b(x, w, tm):
    rows = x.shape[0]
    rmap = lambda i: (i, 0)
    rspec = pl.BlockSpec((tm, D_MODEL), rmap)
    return pl.pallas_call(
        functools.partial(_qkv_b_kernel, tm=tm),
        grid=(rows // tm,),
        in_specs=[rspec, pl.BlockSpec(w.shape, lambda i: (0, 0))],
        out_specs=[rspec, rspec, rspec,
                   pl.BlockSpec((N_HEADS, HD_B, tm), lambda i: (0, 0, i)),
                   rspec,
                   pl.BlockSpec((tm // BLOCK, 1, D_MODEL), lambda i: (i, 0, 0))],
        out_shape=[jax.ShapeDtypeStruct((rows, D_MODEL), BF16),
                   jax.ShapeDtypeStruct((rows, D_MODEL), F32),
                   jax.ShapeDtypeStruct((rows, D_MODEL), F32),
                   jax.ShapeDtypeStruct((N_HEADS, HD_B, rows), BF16),
                   jax.ShapeDtypeStruct((rows, D_MODEL), BF16),
                   jax.ShapeDtypeStruct((rows // BLOCK, 1, D_MODEL), F32)],
        compiler_params=_cparams(1),
        name="moba_qkv_proj",
    )(x, w)


def _out_ln_kernel(a_ref, w_ref, x_ref, lng_ref, lnb_ref, y_ref):
    y = _dot(a_ref[...].astype(BF16), w_ref[...])
    y_ref[...] = _layer_norm(ALPHA * x_ref[...] + y, lng_ref[...], lnb_ref[...])


def _out_ln(a, w, x, lng, lnb, tm):
    rows = x.shape[0]
    rspec = pl.BlockSpec((tm, D_MODEL), lambda i: (i, 0))
    cmap = lambda i: (0, 0)
    return pl.pallas_call(
        _out_ln_kernel,
        grid=(rows // tm,),
        in_specs=[rspec, pl.BlockSpec(w.shape, cmap), rspec,
                  pl.BlockSpec(lng.shape, cmap), pl.BlockSpec(lnb.shape, cmap)],
        out_specs=rspec,
        out_shape=jax.ShapeDtypeStruct((rows, D_MODEL), F32),
        compiler_params=_cparams(1),
        name="attn_out_ln",
    )(a, w, x, lng, lnb)


def _head_slopes(rep):
    slopes = np.zeros((1, LANES), np.float32)
    slopes[0, :N_HEADS * rep] = np.repeat(2.0 ** -np.arange(1, N_HEADS + 1, dtype=np.float64), rep)
    return jnp.asarray(slopes)


def _head_block_diag(rows_8):
    tiled = jnp.concatenate([rows_8] * (LANES // 8), axis=0)
    r = lax.broadcasted_iota(jnp.int32, (LANES, D_MODEL), 0)
    c = lax.broadcasted_iota(jnp.int32, (LANES, D_MODEL), 1)
    keep = (r // 8) == (c // HD_B)
    return jnp.where(keep, tiled, 0.0).astype(BF16)


def _moba_prompt_kernel(qt_ref, kn_ref, q_ref, kt_ref, vb_ref, mean_ref, slope_ref, o_ref,
                        sel_s, mask_s, m_s, l_s, acc_s, s_s):
    p = pl.program_id(1)
    qt = qt_ref[p]
    kn = kn_ref[p]
    scale = HD_B ** -0.5
    rows = Q_TILE
    nqb = Q_TILE // BLOCK
    row_blk = lax.broadcasted_iota(jnp.int32, (rows, LANES), 0) // BLOCK + nqb * qt

    @pl.when(jnp.logical_and(pl.program_id(0) == 0, p == 0))
    def _():
        t_i = lax.broadcasted_iota(jnp.int32, (BLOCK, BLOCK), 0)
        j_i = lax.broadcasted_iota(jnp.int32, (BLOCK, BLOCK), 1)
        jf = j_i.astype(F32)
        for h in range(N_HEADS):
            mask_s[0, h] = (2.0 ** -(h + 1)) * jf
            mask_s[1, h] = jnp.where(j_i > t_i, NEG_INF, (2.0 ** -(h + 1)) * jf)

    @pl.when(kn == 0)
    def _():
        mbd = _head_block_diag(mean_ref[0])
        sc = _dot_nt(q_ref[...], mbd)
        lane = lax.broadcasted_iota(jnp.int32, sc.shape, 1)
        blk = lane % 8
        past = jnp.logical_and(lane < N_HEADS * 8, blk < row_blk)
        sm = jnp.where(past, sc, NEG_INF)
        rank = jnp.zeros(sc.shape, F32)
        for r in range(1, 8):
            lo = pltpu.roll(sm, r, 1)
            rank += jnp.where(blk >= r, jnp.where(lo >= sm, 1.0, 0.0), 0.0)
            hi = pltpu.roll(sm, LANES - r, 1)
            rank += jnp.where(blk + r < 8, jnp.where(hi > sm, 1.0, 0.0), 0.0)
        sel_s[...] = jnp.where(past, jnp.where(rank < TOPK, 1.0, 0.0), 0.0)
        m_s[...] = jnp.full(m_s.shape, M_FLOOR, F32)
        l_s[...] = jnp.zeros_like(l_s)
        acc_s[...] = jnp.zeros_like(acc_s)

    er = lax.broadcasted_iota(jnp.int32, (LANES, LANES), 0)
    ec = lax.broadcasted_iota(jnp.int32, (LANES, LANES), 1)
    pick = jnp.where(jnp.logical_and(er == ec * 8 + kn, ec < N_HEADS), 1.0, 0.0).astype(BF16)
    selcols = _dot(sel_s[...].astype(BF16), pick)
    kbase = slope_ref[...] * (kn * BLOCK).astype(F32)
    attend = jnp.logical_or(selcols > 0.5, row_blk == kn)
    biasc = jnp.where(attend, kbase, NEG_INF)
    ones_v = jnp.ones((BLOCK, HD_B), BF16)
    own_i = [(kn == nqb * qt + j).astype(jnp.int32) for j in range(nqb)]

    for h in range(N_HEADS):
        cs = slice(h * HD_B, (h + 1) * HD_B)
        s_s[h] = _dot(q_ref[:, cs], kt_ref[h])

    for h in range(N_HEADS):
        cs = slice(h * HD_B, (h + 1) * HD_B)
        vaug = jnp.concatenate([vb_ref[:, cs], ones_v], axis=1)
        bias = jnp.broadcast_to(biasc[:, h:h + 1], (rows, LANES))
        mask = jnp.concatenate([mask_s[own_i[j], h] for j in range(nqb)], axis=0)
        lg = s_s[h] * scale + mask + jnp.concatenate([bias, bias], axis=1)
        m_old = m_s[h]
        m_new = jnp.maximum(m_old, jnp.max(lg, axis=1, keepdims=True))
        alpha = jnp.exp(m_old - m_new)
        pr = jnp.exp(lg - jnp.concatenate([m_new, m_new], axis=1))
        pv = _dot(pr.astype(BF16), vaug)
        l_s[h] = alpha * l_s[h] + pv[:, HD_B:]
        acc_s[:, cs] = alpha * acc_s[:, cs] + pv[:, :HD_B]
        m_s[h] = m_new

    @pl.when(kn == nqb * qt + nqb - 1)
    def _():
        for h in range(N_HEADS):
            cs = slice(h * HD_B, (h + 1) * HD_B)
            o_ref[:, cs] = (acc_s[:, cs] / l_s[h]).astype(BF16)


def _moba_prompt(q, kt, vb, means, *, batch, seq):
    nb = seq // BLOCK
    n_qt = seq // Q_TILE
    nqb = Q_TILE // BLOCK
    qt_l, kn_l = [], []
    for i in range(n_qt):
        qt_l += [i] * (nqb * i + nqb)
        kn_l += list(range(nqb * i + nqb))
    qt_arr = jnp.asarray(np.array(qt_l, np.int32))
    kn_arr = jnp.asarray(np.array(kn_l, np.int32))
    grid_spec = pltpu.PrefetchScalarGridSpec(
        num_scalar_prefetch=2,
        grid=(batch, len(qt_l)),
        in_specs=[pl.BlockSpec((Q_TILE, D_MODEL), lambda b, p, qt, kn: (b * n_qt + qt[p], 0)),
                  pl.BlockSpec((N_HEADS, HD_B, BLOCK), lambda b, p, qt, kn: (0, 0, b * nb + kn[p])),
                  pl.BlockSpec((BLOCK, D_MODEL), lambda b, p, qt, kn: (b * nb + kn[p], 0)),
                  pl.BlockSpec((1, nb, D_MODEL), lambda b, p, qt, kn: (b, 0, 0)),
                  pl.BlockSpec((1, LANES), lambda b, p, qt, kn: (0, 0))],
        out_specs=pl.BlockSpec((Q_TILE, D_MODEL), lambda b, p, qt, kn: (b * n_qt + qt[p], 0)),
        scratch_shapes=[pltpu.VMEM((Q_TILE, LANES), F32),
                        pltpu.VMEM((2, N_HEADS, BLOCK, BLOCK), F32),
                        pltpu.VMEM((N_HEADS, Q_TILE, LANES), F32),
                        pltpu.VMEM((N_HEADS, Q_TILE, LANES), F32),
                        pltpu.VMEM((Q_TILE, D_MODEL), F32),
                        pltpu.VMEM((N_HEADS, Q_TILE, BLOCK), F32)])
    return pl.pallas_call(
        _moba_prompt_kernel,
        grid_spec=grid_spec,
        out_shape=jax.ShapeDtypeStruct((batch * seq, D_MODEL), BF16),
        compiler_params=_cparams(2),
        name="moba_prompt_attn",
    )(qt_arr, kn_arr, q, kt, vb, means, _head_slopes(1))


PAGES_PER_STEP = 16


def _moba_sample_kernel(pt_ref, q_ref, kn_ref, vn_ref, *refs, n_pages, dec_seq):
    kp_refs = refs[:PAGES_PER_STEP]
    vp_refs = refs[PAGES_PER_STEP:2 * PAGES_PER_STEP]
    o_ref = refs[2 * PAGES_PER_STEP]
    lt_s, po_s, rden_s, acc_s = refs[2 * PAGES_PER_STEP + 1:]
    ph = pl.program_id(1)
    s = pl.program_id(2)
    n_steps = pl.num_programs(2)
    past_len = n_pages * PAGE
    n_blocks = past_len // BLOCK
    ppb = BLOCK // PAGE
    scale = HD_B ** -0.5

    def head_rows(page_ref, h):
        return page_ref[0, pl.ds(h, PAGE, stride=N_HEADS), :]

    def pad_page(new_ref, h):
        blk = new_ref[:, h * HD_B:(h + 1) * HD_B]
        return jnp.concatenate([blk, jnp.zeros((PAGE - dec_seq, HD_B), F32)], axis=0).astype(BF16)

    def q_head(h):
        return q_ref[:, h * HD_B:(h + 1) * HD_B].astype(BF16)

    @pl.when(ph == 0)
    def _():
        for i in range(PAGES_PER_STEP):
            pg = s * PAGES_PER_STEP + i
            for h in range(N_HEADS):
                k_t = head_rows(kp_refs[i], h).T.astype(BF16)
                lt_s[h, pg] = _dot(q_head(h), k_t)

    @pl.when(jnp.logical_and(ph == 0, s == n_steps - 1))
    def _():
        lane = lax.broadcasted_iota(jnp.int32, (dec_seq, LANES), 1)
        rows = []
        for h in range(N_HEADS):
            blk_sum = jnp.sum(lt_s[h].reshape(n_blocks, ppb, dec_seq, PAGE), axis=1)
            tot = jnp.sum(blk_sum, axis=2, keepdims=True) * (1.0 / BLOCK)
            sc_h = jnp.full((dec_seq, LANES), NEG_INF, F32)
            for n in range(n_blocks):
                sc_h = jnp.where(lane == n, tot[n], sc_h)
            rows.append(sc_h)
        sc = jnp.concatenate(rows, axis=0)
        lane_a = lax.broadcasted_iota(jnp.int32, sc.shape, 1)
        rank = jnp.zeros(sc.shape, F32)
        for r in range(1, n_blocks):
            lo = pltpu.roll(sc, r, 1)
            rank += jnp.where(lane_a >= r, jnp.where(lo >= sc, 1.0, 0.0), 0.0)
            hi = pltpu.roll(sc, LANES - r, 1)
            rank += jnp.where(lane_a + r < n_blocks, jnp.where(hi > sc, 1.0, 0.0), 0.0)
        sel_bias = jnp.where(jnp.logical_and(rank < TOPK, lane_a < n_blocks), 0.0, NEG_INF)
        kpos = (lax.broadcasted_iota(jnp.int32, (n_pages, 1, PAGE), 0) * PAGE
                + lax.broadcasted_iota(jnp.int32, (n_pages, 1, PAGE), 2) - past_len).astype(F32)
        qi = lax.broadcasted_iota(jnp.int32, (dec_seq, PAGE), 0)
        ki = lax.broadcasted_iota(jnp.int32, (dec_seq, PAGE), 1)
        for h in range(N_HEADS):
            slope = 2.0 ** -(h + 1)
            bias_h = sel_bias[h * dec_seq:(h + 1) * dec_seq, :]
            bias_pages = jnp.stack([jnp.broadcast_to(bias_h[:, n:n + 1], (dec_seq, PAGE))
                                    for n in range(n_blocks) for _ in range(ppb)], axis=0)
            lgs = lt_s[h] * scale + slope * kpos + bias_pages
            lgo = _dot_nt(q_head(h), pad_page(kn_ref, h))
            lgo = jnp.where(ki <= qi, lgo * scale + slope * ki.astype(F32), NEG_INF)
            mx = jnp.maximum(jnp.max(jnp.max(lgs, axis=0), axis=1, keepdims=True),
                             jnp.max(lgo, axis=1, keepdims=True))
            pr = jnp.exp(lgs - mx[None])
            po = jnp.exp(lgo - mx)
            den = jnp.sum(jnp.sum(pr, axis=0), axis=1, keepdims=True) + jnp.sum(po, axis=1, keepdims=True)
            lt_s[h] = pr
            po_s[h] = po
            rden_s[h] = jnp.broadcast_to(1.0 / den, (dec_seq, LANES))

    @pl.when(jnp.logical_and(ph == 1, s == 0))
    def _():
        for h in range(N_HEADS):
            acc_s[:, h * HD_B:(h + 1) * HD_B] = _dot(po_s[h].astype(BF16), pad_page(vn_ref, h))

    @pl.when(ph == 1)
    def _():
        for i in range(PAGES_PER_STEP):
            pg = s * PAGES_PER_STEP + i
            for h in range(N_HEADS):
                acc_s[:, h * HD_B:(h + 1) * HD_B] += _dot(lt_s[h, pg].astype(BF16),
                                                          head_rows(vp_refs[i], h).astype(BF16))

    @pl.when(jnp.logical_and(ph == 1, s == n_steps - 1))
    def _():
        for h in range(N_HEADS):
            cs = slice(h * HD_B, (h + 1) * HD_B)
            o_ref[:, cs] = acc_s[:, cs] * rden_s[h]


def _moba_sample(page_table, q, k_new, v_new, cache_k, cache_v, *, batch, dec_seq):
    n_pages = page_table.shape[1]
    n_steps = n_pages // PAGES_PER_STEP
    rspec = pl.BlockSpec((dec_seq, D_MODEL), lambda b, ph, s, pt: (b, 0))

    def kmap(i):
        return lambda b, ph, s, pt: (pt[b, jnp.where(ph == 0, s, n_steps - 1) * PAGES_PER_STEP + i], 0, 0)

    def vmap_(i):
        return lambda b, ph, s, pt: (pt[b, jnp.where(ph == 0, 0, s) * PAGES_PER_STEP + i], 0, 0)

    pspec = lambda m: pl.BlockSpec((1, PAGE * N_HEADS, HD_B), m)
    grid_spec = pltpu.PrefetchScalarGridSpec(
        num_scalar_prefetch=1,
        grid=(batch, 2, n_steps),
        in_specs=[rspec, rspec, rspec]
                 + [pspec(kmap(i)) for i in range(PAGES_PER_STEP)]
                 + [pspec(vmap_(i)) for i in range(PAGES_PER_STEP)],
        out_specs=rspec,
        scratch_shapes=[pltpu.VMEM((N_HEADS, n_pages, dec_seq, PAGE), F32),
                        pltpu.VMEM((N_HEADS, dec_seq, PAGE), F32),
                        pltpu.VMEM((N_HEADS, dec_seq, LANES), F32),
                        pltpu.VMEM((dec_seq, D_MODEL), F32)])
    kern = functools.partial(_moba_sample_kernel, n_pages=n_pages, dec_seq=dec_seq)
    return pl.pallas_call(
        kern,
        grid_spec=grid_spec,
        out_shape=jax.ShapeDtypeStruct((batch * dec_seq, D_MODEL), F32),
        compiler_params=_cparams(3),
        name="moba_sample_attn",
    )(page_table, q, k_new, v_new, *([cache_k] * PAGES_PER_STEP), *([cache_v] * PAGES_PER_STEP))


def kernel(x_prompt, x_sample, state_C, state_n, state_m, cache_k, cache_v, page_table,
           w_in_a, b_gate_a, w_out_a, w_q_b, w_out_b, w_kv, w_up, w_down, ln_g, ln_b):
    bp, sp, _ = x_prompt.shape
    bs, ss, _ = x_sample.shape
    ch = MLSTM_CHUNK
    nq = 2 * QK_W + 2 * V_W

    wm_a = w_in_a[0, :, :nq].astype(BF16)
    wg_a = jnp.pad(w_in_a[0, :, nq:], ((0, 0), (0, LANES - 2 * N_HEADS)))
    bg_a = jnp.pad(b_gate_a[0], (0, LANES - 2 * N_HEADS)).reshape(1, LANES)
    wout_a = w_out_a[0].astype(BF16)
    w_qkv_b = jnp.concatenate([w_q_b[0], w_kv], axis=1).astype(BF16)
    wout_b = w_out_b[0].astype(BF16)
    wup = w_up.astype(BF16)
    wdn = w_down.astype(BF16)
    lng = ln_g.reshape(DEPTH, 2, 1, D_MODEL)
    lnb = ln_b.reshape(DEPTH, 2, 1, D_MODEL)

    xp = x_prompt.reshape(bp * sp, D_MODEL)
    qv, kt, o, gi, gf, gt = _proj_a(xp, wm_a, wg_a, bg_a, 512, ch)
    c0 = jnp.zeros((bp, N_HEADS, DK_A, DV_A), F32)
    n0 = jnp.zeros((bp, N_HEADS, DK_A), F32)
    m0 = jnp.zeros((bp, 1, LANES), F32)
    x1, c_p, n_p, m_p = _mlstm(qv, kt, o, gi, gf, gt, xp, c0, n0, m0, wout_a, lng[0, 0], lnb[0, 0],
                             batch=bp, seq=sp, tc=512, ch=ch)
    x2 = _mlp(x1, wup[0], wdn[0], lng[0, 1], lnb[0, 1], 1024, 1024)
    q, k, v, kt, vb, means = _qkv_b(x2, w_qkv_b, 512)
    att = _moba_prompt(q, kt, vb, means.reshape(bp, sp // BLOCK, D_MODEL), batch=bp, seq=sp)
    x3 = _out_ln(att, wout_b, x2, lng[1, 0], lnb[1, 0], 512)
    y_prompt = _mlp(x3, wup[1], wdn[1], lng[1, 1], lnb[1, 1], 1024, 1024)

    rows_s = bs * ss
    xs = x_sample.reshape(rows_s, D_MODEL)
    qv_s, kt_s, o_s, gi_s, gf_s, _ = _proj_a(xs, wm_a, wg_a, bg_a, rows_s, ch)
    pad3 = lambda a, val=0.0: jnp.pad(a.reshape(bs, ss, -1), ((0, 0), (0, ch - ss), (0, 0)),
                                      constant_values=val)
    flat = lambda a: a.reshape(bs * ch, -1)
    gi_pad = jnp.where(jnp.arange(LANES)[None, None, :] < N_HEADS, pad3(gi_s, NEG_INF), 0.0)
    gf_pad = pad3(gf_s)
    gt_pad = jnp.concatenate([jnp.swapaxes(gi_pad, 1, 2)[:, :N_HEADS],
                              jnp.swapaxes(gf_pad, 1, 2)[:, :N_HEADS]], axis=1)
    m0s = jnp.pad(state_m[0], ((0, 0), (0, LANES - N_HEADS))).reshape(bs, 1, LANES)
    kt_pad = jnp.pad(kt_s.reshape(QK_W, bs, ss), ((0, 0), (0, 0), (0, ch - ss))).reshape(QK_W, bs * ch)
    x1s_pad, c_s, n_s, m_s = _mlstm(flat(pad3(qv_s)), kt_pad, flat(pad3(o_s)), flat(gi_pad), flat(gf_pad), gt_pad,
                                    flat(pad3(xs)), state_C[0], state_n[0], m0s, wout_a, lng[0, 0], lnb[0, 0],
                                  batch=bs, seq=ch, tc=ch, ch=ch)
    x1s = x1s_pad.reshape(bs, ch, D_MODEL)[:, :ss].reshape(rows_s, D_MODEL)
    x2s = _mlp(x1s, wup[0], wdn[0], lng[0, 1], lnb[0, 1], rows_s, 512)
    q_s, k_s, v_s, _, _, _ = _qkv_b(x2s, w_qkv_b, rows_s)
    n_pool = cache_k.shape[0]
    att_s = _moba_sample(page_table, q_s.astype(F32), k_s, v_s,
                         cache_k.reshape(n_pool, PAGE * N_HEADS, HD_B),
                         cache_v.reshape(n_pool, PAGE * N_HEADS, HD_B), batch=bs, dec_seq=ss)
    x3s = _out_ln(att_s, wout_b, x2s, lng[1, 0], lnb[1, 0], rows_s)
    y_sample = _mlp(x3s, wup[1], wdn[1], lng[1, 1], lnb[1, 1], rows_s, 512)

    p_c, p_n, p_m = c_p[None], n_p[None], m_p[None, :, 0, :N_HEADS]
    s_c, s_n, s_m = c_s[None], n_s[None], m_s[None, :, 0, :N_HEADS]
    return (y_prompt.reshape(bp, sp, D_MODEL), y_sample.reshape(bs, ss, D_MODEL),
            p_c, p_n, p_m,
            k.reshape(bp, sp, N_HEADS, HD_B), v.reshape(bp, sp, N_HEADS, HD_B),
            s_c, s_n, s_m,
            k_s.reshape(bs, ss, N_HEADS, HD_B), v_s.reshape(bs, ss, N_HEADS, HD_B))
```

```python
import functools

import numpy as np
import jax
import jax.numpy as jnp
from jax import lax
from jax.experimental import pallas as pl
from jax.experimental.pallas import tpu as pltpu

F32 = jnp.float32
BF16 = jnp.bfloat16

D_MODEL = 1024
N_HEADS = 8
DK_A = 64
DV_A = 128
QK_W = N_HEADS * DK_A
V_W = N_HEADS * DV_A
HD_B = 128
BLOCK = 256
TOPK = 3
PAGE = 128
D_FF = 4 * D_MODEL
DEPTH = 2
ALPHA = (2.0 * DEPTH) ** 0.25
LN_EPS = 1e-5
NEG_INF = float("-inf")

LANES = 128
VMEM_LIMIT = 56 * 1024 * 1024

MLSTM_CHUNK = 128
M_FLOOR = -1e30


def _cparams(n_axes, flags=None):
    return pltpu.CompilerParams(dimension_semantics=("arbitrary",) * n_axes,
                                vmem_limit_bytes=VMEM_LIMIT, flags=flags)


def _dot(a, b):
    return jnp.dot(a, b, preferred_element_type=F32)


def _dot_nt(a, b):
    return lax.dot_general(a, b, (((1,), (1,)), ((), ())), preferred_element_type=F32)


def _dot_tn(a, b):
    return lax.dot_general(a, b, (((0,), (0,)), ((), ())), preferred_element_type=F32)


def _split3(x):
    x1 = x.astype(BF16)
    r1 = x - x1.astype(F32)
    x2 = r1.astype(BF16)
    x3 = (r1 - x2.astype(F32)).astype(BF16)
    return x1, x2, x3


def _layer_norm(y, g, b):
    mu = jnp.mean(y, axis=-1, keepdims=True)
    d = y - mu
    var = jnp.mean(d * d, axis=-1, keepdims=True)
    return d * lax.rsqrt(var + LN_EPS) * g + b


def _cummax_rows(x):
    row = lax.broadcasted_iota(jnp.int32, x.shape, 0)
    shift = 1
    while shift < x.shape[0]:
        x = jnp.maximum(x, jnp.where(row >= shift, pltpu.roll(x, shift, 0), NEG_INF))
        shift *= 2
    return x


def _proj_a_kernel(x_ref, wm_ref, wg_ref, bg_ref, qv_ref, kt_ref, o_ref, gi_ref, gf_ref, gt_ref, *, ch):
    x = x_ref[...]
    xh = x.astype(BF16)
    nq = 2 * QK_W + V_W
    step = 512
    for c0 in range(0, nq, step):
        acc = _dot(xh, wm_ref[:, c0:c0 + step])
        if c0 == QK_W:
            kt_ref[...] = (acc * (DK_A ** -0.5)).T.astype(BF16)
        else:
            dst = c0 if c0 < QK_W else c0 - QK_W
            qv_ref[:, dst:dst + step] = acc.astype(BF16)
    for c0 in range(0, V_W, step):
        o_ref[:, c0:c0 + step] = _dot(xh, wm_ref[:, nq + c0:nq + c0 + step])
    xl = (x - xh.astype(F32)).astype(BF16)
    wg = wg_ref[...]
    wgh = wg.astype(BF16)
    wgl = (wg - wgh.astype(F32)).astype(BF16)
    g = _dot(xh, wgh) + _dot(xl, wgh) + _dot(xh, wgl) + bg_ref[...]
    lane = lax.broadcasted_iota(jnp.int32, g.shape, 1)
    log_f = jnp.minimum(g, 0.0) - jnp.log1p(jnp.exp(-jnp.abs(g)))
    gi_ref[...] = jnp.where(lane < N_HEADS, g, 0.0)
    gf_ref[...] = jnp.where(lane < N_HEADS, pltpu.roll(log_f, LANES - N_HEADS, 1), 0.0)
    both_t = jnp.where(lane < N_HEADS, g, jnp.where(lane < 2 * N_HEADS, log_f, 0.0)).T
    for c in range(x.shape[0] // ch):
        gt_ref[c] = both_t[0:2 * N_HEADS, c * ch:(c + 1) * ch]


def _proj_a(x, wm, wg, bg, tm, ch):
    rows = x.shape[0]
    return pl.pallas_call(
        functools.partial(_proj_a_kernel, ch=ch),
        grid=(rows // tm,),
        in_specs=[pl.BlockSpec((tm, D_MODEL), lambda i: (i, 0)),
                  pl.BlockSpec(wm.shape, lambda i: (0, 0)),
                  pl.BlockSpec(wg.shape, lambda i: (0, 0)),
                  pl.BlockSpec(bg.shape, lambda i: (0, 0))],
        out_specs=[pl.BlockSpec((tm, QK_W + V_W), lambda i: (i, 0)),
                   pl.BlockSpec((QK_W, tm), lambda i: (0, i)),
                   pl.BlockSpec((tm, V_W), lambda i: (i, 0)),
                   pl.BlockSpec((tm, LANES), lambda i: (i, 0)),
                   pl.BlockSpec((tm, LANES), lambda i: (i, 0)),
                   pl.BlockSpec((tm // ch, 2 * N_HEADS, ch), lambda i: (i, 0, 0))],
        out_shape=[jax.ShapeDtypeStruct((rows, QK_W + V_W), BF16),
                   jax.ShapeDtypeStruct((QK_W, rows), BF16),
                   jax.ShapeDtypeStruct((rows, V_W), F32),
                   jax.ShapeDtypeStruct((rows, LANES), F32),
                   jax.ShapeDtypeStruct((rows, LANES), F32),
                   jax.ShapeDtypeStruct((rows // ch, 2 * N_HEADS, ch), F32)],
        compiler_params=_cparams(1),
        name="mlstm_in_proj",
    )(x, wm, wg, bg)


def _mlstm_kernel(qv_ref, kt_ref, o_ref, gi_ref, gf_ref, gt_ref, x_ref, c0_ref, n0_ref, m0_ref, wout_ref,
                  lng_ref, lnb_ref, y_ref, ct_ref, nt_ref, mt_ref, caug_s, m_s, h_s, *, tc, ch):
    t = pl.program_id(1)
    eye_k = (lax.broadcasted_iota(jnp.int32, (DK_A, DK_A), 0)
             == lax.broadcasted_iota(jnp.int32, (DK_A, DK_A), 1))

    @pl.when(t == 0)
    def _():
        for h in range(N_HEADS):
            caug_s[h, :, :DV_A] = c0_ref[0, h]
            n_row = jnp.broadcast_to(n0_ref[0, h:h + 1, :], (DK_A, DK_A))
            n_col = jnp.sum(jnp.where(eye_k, n_row, 0.0), axis=1, keepdims=True)
            caug_s[h, :, DV_A:] = jnp.broadcast_to(n_col, (DK_A, DV_A))
        m_s[...] = m0_ref[0]

    row = lax.broadcasted_iota(jnp.int32, (ch, ch), 0)
    col = lax.broadcasted_iota(jnp.int32, (ch, ch), 1)
    tril = row >= col
    tri_b = jnp.where(tril, 1.0, 0.0).astype(BF16)
    tri_t = jnp.where(row <= col, 1.0, 0.0).astype(BF16)
    ones_blk = jnp.ones((ch, LANES), BF16)
    trow = lax.broadcasted_iota(jnp.int32, (2 * N_HEADS, ch), 0)

    for j in range(tc // ch):
        r0 = j * ch
        gi = gi_ref[r0:r0 + ch, :]
        f1, f2, f3 = _split3(gf_ref[r0:r0 + ch, :])
        b = _dot(tri_b, f1) + _dot(tri_b, f2) + _dot(tri_b, f3)
        u = gi - b
        m_prev = m_s[...]
        big_m = jnp.maximum(_cummax_rows(u), m_prev)
        a_all = jnp.exp(m_prev - big_m)
        e_all = jnp.exp(-(b + big_m))
        m_end = big_m[ch - 1:ch, :]
        ae_all = a_all[ch - 1:ch, :]
        gt = gt_ref[j]
        r1, r2, r3 = _split3(jnp.where(trow >= N_HEADS, gt, 0.0))
        b_t = _dot(r1, tri_t) + _dot(r2, tri_t) + _dot(r3, tri_t)
        u_t = gt[0:N_HEADS, :] - b_t[N_HEADS:2 * N_HEADS, :]
        for h in range(N_HEADS):
            w = jnp.exp(jnp.where(tril, u_t[h:h + 1, :] - big_m[:, h:h + 1], NEG_INF))
            q = qv_ref[r0:r0 + ch, h * DK_A:(h + 1) * DK_A]
            k_t = kt_ref[h * DK_A:(h + 1) * DK_A, r0:r0 + ch]
            v = qv_ref[r0:r0 + ch, QK_W + h * DV_A:QK_W + (h + 1) * DV_A]
            vaug = jnp.concatenate([v, ones_blk], axis=1)
            sw = _dot(q, k_t) * w
            qc = _dot(q, caug_s[h].astype(BF16))
            a = jnp.broadcast_to(a_all[:, h:h + 1], (ch, LANES))
            tot = _dot(sw.astype(BF16), vaug) + jnp.concatenate([a, a], axis=1) * qc
            floor = jnp.broadcast_to(e_all[:, h:h + 1], (ch, LANES))
            rden = 1.0 / jnp.maximum(jnp.abs(tot[:, DV_A:]), floor)
            og = o_ref[r0:r0 + ch, h * DV_A:(h + 1) * DV_A]
            hh = tot[:, :DV_A] * rden * (1.0 / (1.0 + jnp.exp(-og)))
            h_s[r0:r0 + ch, h * DV_A:(h + 1) * DV_A] = hh.astype(BF16)
            ws_t = jnp.exp(u_t[h:h + 1, :] - m_end[:, h:h + 1])
            kw_t = (k_t.astype(F32) * ws_t).astype(BF16)
            caug_s[h] = ae_all[:, h:h + 1] * caug_s[h] + _dot(kw_t, vaug)
        m_s[...] = b[ch - 1:ch, :] + m_end

    y = _dot(h_s[...], wout_ref[...])
    y_ref[...] = _layer_norm(ALPHA * x_ref[...] + y, lng_ref[...], lnb_ref[...])

    @pl.when(t == pl.num_programs(1) - 1)
    def _():
        for h in range(N_HEADS):
            ct_ref[0, h] = caug_s[h, :, :DV_A]
            n_col = jnp.broadcast_to(caug_s[h, :, DV_A:DV_A + 1], (DK_A, DK_A))
            nt_ref[0, h:h + 1, :] = jnp.sum(jnp.where(eye_k, n_col, 0.0), axis=0, keepdims=True)
        mt_ref[0] = m_s[...]


def _mlstm(qv, kt, o, gi, gf, gt, x, c0, n0, m0, wout, lng, lnb, *, batch, seq, tc, ch):
    n_t = seq // tc
    rows = batch * seq
    kern = functools.partial(_mlstm_kernel, tc=tc, ch=ch)
    rmap = lambda b, t: (b * n_t + t, 0)
    cmap = lambda b, t: (0, 0)
    smap = lambda b, t: (b, 0, 0, 0)
    mmap = lambda b, t: (b, 0, 0)
    return pl.pallas_call(
        kern,
        grid=(batch, n_t),
        in_specs=[pl.BlockSpec((tc, QK_W + V_W), rmap),
                  pl.BlockSpec((QK_W, tc), lambda b, t: (0, b * n_t + t)),
                  pl.BlockSpec((tc, V_W), rmap),
                  pl.BlockSpec((tc, LANES), rmap),
                  pl.BlockSpec((tc, LANES), rmap),
                  pl.BlockSpec((tc // ch, 2 * N_HEADS, ch), lambda b, t: (b * n_t + t, 0, 0)),
                  pl.BlockSpec((tc, D_MODEL), rmap),
                  pl.BlockSpec((1, N_HEADS, DK_A, DV_A), smap),
                  pl.BlockSpec((1, N_HEADS, DK_A), mmap),
                  pl.BlockSpec((1, 1, LANES), mmap),
                  pl.BlockSpec(wout.shape, cmap),
                  pl.BlockSpec(lng.shape, cmap),
                  pl.BlockSpec(lnb.shape, cmap)],
        out_specs=[pl.BlockSpec((tc, D_MODEL), rmap),
                   pl.BlockSpec((1, N_HEADS, DK_A, DV_A), smap),
                   pl.BlockSpec((1, N_HEADS, DK_A), mmap),
                   pl.BlockSpec((1, 1, LANES), mmap)],
        out_shape=[jax.ShapeDtypeStruct((rows, D_MODEL), F32),
                   jax.ShapeDtypeStruct((batch, N_HEADS, DK_A, DV_A), F32),
                   jax.ShapeDtypeStruct((batch, N_HEADS, DK_A), F32),
                   jax.ShapeDtypeStruct((batch, 1, LANES), F32)],
        scratch_shapes=[pltpu.VMEM((N_HEADS, DK_A, 2 * DV_A), F32),
                        pltpu.VMEM((1, LANES), F32),
                        pltpu.VMEM((tc, V_W), BF16)],
        compiler_params=_cparams(2),
        name="mlstm_scan",
    )(qv, kt, o, gi, gf, gt, x, c0, n0, m0, wout, lng, lnb)


def _mlp_kernel(x_ref, wup_ref, wdn_ref, lng_ref, lnb_ref, y_ref, acc_s, xb_s):
    j = pl.program_id(1)

    @pl.when(j == 0)
    def _():
        xb_s[...] = x_ref[...].astype(BF16)
        acc_s[...] = jnp.zeros_like(acc_s)

    hid = jnp.maximum(_dot(xb_s[...], wup_ref[...]), 0.0)
    acc_s[...] += _dot((hid * hid).astype(BF16), wdn_ref[...])

    @pl.when(j == pl.num_programs(1) - 1)
    def _():
        y_ref[...] = _layer_norm(ALPHA * x_ref[...] + acc_s[...], lng_ref[...], lnb_ref[...])


def _mlp(x, wup, wdn, lng, lnb, tm, tf, name="mlp_ln"):
    rows = x.shape[0]
    return pl.pallas_call(
        _mlp_kernel,
        grid=(rows // tm, D_FF // tf),
        in_specs=[pl.BlockSpec((tm, D_MODEL), lambda i, j: (i, 0)),
                  pl.BlockSpec((D_MODEL, tf), lambda i, j: (0, j)),
                  pl.BlockSpec((tf, D_MODEL), lambda i, j: (j, 0)),
                  pl.BlockSpec(lng.shape, lambda i, j: (0, 0)),
                  pl.BlockSpec(lnb.shape, lambda i, j: (0, 0))],
        out_specs=pl.BlockSpec((tm, D_MODEL), lambda i, j: (i, 0)),
        out_shape=jax.ShapeDtypeStruct((rows, D_MODEL), F32),
        scratch_shapes=[pltpu.VMEM((tm, D_MODEL), F32), pltpu.VMEM((tm, D_MODEL), BF16)],
        compiler_params=_cparams(2),
        name=name,
    )(x, wup, wdn, lng, lnb)


def _qkv_b_kernel(x_ref, w_ref, q_ref, k_ref, v_ref, kt_ref, vb_ref, mean_ref, *, tm):
    xh = x_ref[...].astype(BF16)
    step = 512
    for c0 in range(0, D_MODEL, step):
        q_ref[:, c0:c0 + step] = _dot(xh, w_ref[:, c0:c0 + step]).astype(BF16)
        kk = _dot(xh, w_ref[:, D_MODEL + c0:D_MODEL + c0 + step])
        k_ref[:, c0:c0 + step] = kk
        for hh in range(step // HD_B):
            for r in range(tm // BLOCK):
                kt_ref[c0 // HD_B + hh, r] = kk[r * BLOCK:(r + 1) * BLOCK, hh * HD_B:(hh + 1) * HD_B].T.astype(BF16)
        for r in range(tm // BLOCK):
            mean_ref[r, :, c0:c0 + step] = jnp.mean(kk[r * BLOCK:(r + 1) * BLOCK], axis=0, keepdims=True)
        vv = _dot(xh, w_ref[:, 2 * D_MODEL + c0:2 * D_MODEL + c0 + step])
        v_ref[:, c0:c0 + step] = vv
        vb_ref[:, c0:c0 + step] = vv.astype(BF16)


def _qkv_b(x, w, tm):
    rows = x.shape[0]
    rmap = lambda i: (i, 0)
    rspec = pl.BlockSpec((tm, D_MODEL), rmap)
    return pl.pallas_call(
        functools.partial(_qkv_b_kernel, tm=tm),
        grid=(rows // tm,),
        in_specs=[rspec, pl.BlockSpec(w.shape, lambda i: (0, 0))],
        out_specs=[rspec, rspec, rspec,
                   pl.BlockSpec((N_HEADS, tm // BLOCK, HD_B, BLOCK), lambda i: (0, i, 0, 0)),
                   rspec,
                   pl.BlockSpec((tm // BLOCK, 1, D_MODEL), lambda i: (i, 0, 0))],
        out_shape=[jax.ShapeDtypeStruct((rows, D_MODEL), BF16),
                   jax.ShapeDtypeStruct((rows, D_MODEL), F32),
                   jax.ShapeDtypeStruct((rows, D_MODEL), F32),
                   jax.ShapeDtypeStruct((N_HEADS, rows // BLOCK, HD_B, BLOCK), BF16),
                   jax.ShapeDtypeStruct((rows, D_MODEL), BF16),
                   jax.ShapeDtypeStruct((rows // BLOCK, 1, D_MODEL), F32)],
        compiler_params=_cparams(1),
        name="moba_qkv_proj",
    )(x, w)


def _out_ln_kernel(a_ref, w_ref, x_ref, lng_ref, lnb_ref, y_ref):
    y = _dot(a_ref[...].astype(BF16), w_ref[...])
    y_ref[...] = _layer_norm(ALPHA * x_ref[...] + y, lng_ref[...], lnb_ref[...])


def _out_ln(a, w, x, lng, lnb, tm):
    rows = x.shape[0]
    rspec = pl.BlockSpec((tm, D_MODEL), lambda i: (i, 0))
    cmap = lambda i: (0, 0)
    return pl.pallas_call(
        _out_ln_kernel,
        grid=(rows // tm,),
        in_specs=[rspec, pl.BlockSpec(w.shape, cmap), rspec,
                  pl.BlockSpec(lng.shape, cmap), pl.BlockSpec(lnb.shape, cmap)],
        out_specs=rspec,
        out_shape=jax.ShapeDtypeStruct((rows, D_MODEL), F32),
        compiler_params=_cparams(1),
        name="attn_out_ln",
    )(a, w, x, lng, lnb)


def _head_slopes(rep):
    slopes = np.zeros((1, LANES), np.float32)
    slopes[0, :N_HEADS * rep] = np.repeat(2.0 ** -np.arange(1, N_HEADS + 1, dtype=np.float64), rep)
    return jnp.asarray(slopes)


def _head_block_diag(rows_8):
    tiled = jnp.concatenate([rows_8] * (LANES // 8), axis=0)
    r = lax.broadcasted_iota(jnp.int32, (LANES, D_MODEL), 0)
    c = lax.broadcasted_iota(jnp.int32, (LANES, D_MODEL), 1)
    keep = (r // 8) == (c // HD_B)
    return jnp.where(keep, tiled, 0.0).astype(BF16)


def _moba_prompt_kernel(q_ref, kt_ref, vb_ref, mean_ref, slope_ref, x_ref, wout_ref, lng_ref, lnb_ref, o_ref,
                        sel_s, mask_s, m_s, l_s, acc_s, s_s):
    qi = pl.program_id(1)
    scale = HD_B ** -0.5
    rows = BLOCK

    @pl.when(jnp.logical_and(pl.program_id(0) == 0, qi == 0))
    def _():
        t_i = lax.broadcasted_iota(jnp.int32, (BLOCK, BLOCK), 0)
        j_i = lax.broadcasted_iota(jnp.int32, (BLOCK, BLOCK), 1)
        jf = j_i.astype(F32)
        for h in range(N_HEADS):
            mask_s[0, h] = (2.0 ** -(h + 1)) * jf
            mask_s[1, h] = jnp.where(j_i > t_i, NEG_INF, (2.0 ** -(h + 1)) * jf)

    mbd = _head_block_diag(mean_ref[0])
    sc = _dot_nt(q_ref[...], mbd)
    lane = lax.broadcasted_iota(jnp.int32, sc.shape, 1)
    blk = lane % 8
    past = jnp.logical_and(lane < N_HEADS * 8, blk < qi)
    sm = jnp.where(past, sc, NEG_INF)
    rank = jnp.zeros(sc.shape, F32)
    for r in range(1, 8):
        lo = pltpu.roll(sm, r, 1)
        rank += jnp.where(blk >= r, jnp.where(lo >= sm, 1.0, 0.0), 0.0)
        hi = pltpu.roll(sm, LANES - r, 1)
        rank += jnp.where(blk + r < 8, jnp.where(hi > sm, 1.0, 0.0), 0.0)
    sel_s[...] = jnp.where(past, jnp.where(rank < TOPK, 1.0, 0.0), 0.0).astype(BF16)
    m_s[...] = jnp.full(m_s.shape, M_FLOOR, F32)
    l_s[...] = jnp.zeros_like(l_s)
    acc_s[...] = jnp.zeros_like(acc_s)

    er = lax.broadcasted_iota(jnp.int32, (LANES, LANES), 0)
    ec = lax.broadcasted_iota(jnp.int32, (LANES, LANES), 1)
    ones_v = jnp.ones((BLOCK, HD_B), BF16)

    def key_block(kn, carry):
        pick = jnp.where(jnp.logical_and(er == ec * 8 + kn, ec < N_HEADS), 1.0, 0.0).astype(BF16)
        selcols = _dot(sel_s[...], pick)
        kbase = slope_ref[...] * (kn * BLOCK).astype(F32)
        own = kn == qi
        biasc = jnp.where(jnp.logical_or(selcols > 0.5, own), kbase, NEG_INF)
        own_i = own.astype(jnp.int32)
        r0 = pl.multiple_of(kn * BLOCK, BLOCK)

        for h in range(N_HEADS):
            cs = slice(h * HD_B, (h + 1) * HD_B)
            s_s[h] = _dot(q_ref[:, cs], kt_ref[h, kn])

        for h in range(N_HEADS):
            cs = slice(h * HD_B, (h + 1) * HD_B)
            vaug = jnp.concatenate([vb_ref[pl.ds(r0, BLOCK), cs], ones_v], axis=1)
            bias = jnp.broadcast_to(biasc[:, h:h + 1], (rows, LANES))
            lg = s_s[h] * scale + mask_s[own_i, h] + jnp.concatenate([bias, bias], axis=1)
            m_old = m_s[h]
            m_new = jnp.maximum(m_old, jnp.max(lg, axis=1, keepdims=True))
            alpha = jnp.exp(m_old - m_new)
            pr = jnp.exp(lg - jnp.concatenate([m_new, m_new], axis=1))
            pv = _dot(pr.astype(BF16), vaug)
            l_s[h] = alpha * l_s[h] + pv[:, HD_B:]
            acc_s[:, cs] = alpha * acc_s[:, cs] + pv[:, :HD_B]
            m_s[h] = m_new
        return carry

    lax.fori_loop(0, qi + 1, key_block, 0)

    for h in range(N_HEADS):
        cs = slice(h * HD_B, (h + 1) * HD_B)
        acc_s[:, cs] = acc_s[:, cs] / l_s[h]
    y = _dot(acc_s[...].astype(BF16), wout_ref[...])
    o_ref[...] = _layer_norm(ALPHA * x_ref[...] + y, lng_ref[...], lnb_ref[...])


def _moba_prompt(q, kt, vb, means, x, wout, lng, lnb, *, batch, seq):
    nb = seq // BLOCK
    qmap = lambda b, i: (b * nb + i, 0)
    cmap = lambda b, i: (0, 0)
    return pl.pallas_call(
        _moba_prompt_kernel,
        grid=(batch, nb),
        in_specs=[pl.BlockSpec((BLOCK, D_MODEL), qmap),
                  pl.BlockSpec((N_HEADS, nb, HD_B, BLOCK), lambda b, i: (0, b, 0, 0)),
                  pl.BlockSpec((seq, D_MODEL), lambda b, i: (b, 0)),
                  pl.BlockSpec((1, nb, D_MODEL), lambda b, i: (b, 0, 0)),
                  pl.BlockSpec((1, LANES), cmap),
                  pl.BlockSpec((BLOCK, D_MODEL), qmap),
                  pl.BlockSpec(wout.shape, cmap),
                  pl.BlockSpec(lng.shape, cmap),
                  pl.BlockSpec(lnb.shape, cmap)],
        out_specs=pl.BlockSpec((BLOCK, D_MODEL), qmap),
        out_shape=jax.ShapeDtypeStruct((batch * seq, D_MODEL), F32),
        scratch_shapes=[pltpu.VMEM((BLOCK, LANES), BF16),
                        pltpu.VMEM((2, N_HEADS, BLOCK, BLOCK), F32),
                        pltpu.VMEM((N_HEADS, BLOCK, LANES), F32),
                        pltpu.VMEM((N_HEADS, BLOCK, LANES), F32),
                        pltpu.VMEM((BLOCK, D_MODEL), F32),
                        pltpu.VMEM((N_HEADS, BLOCK, BLOCK), F32)],
        compiler_params=_cparams(2),
        name="moba_prompt_attn",
    )(q, kt, vb, means, _head_slopes(1), x, wout, lng, lnb)


PAGES_PER_STEP = 16


def _moba_sample_kernel(pt_ref, q_ref, kn_ref, vn_ref, *refs, n_pages, dec_seq):
    kp_refs = refs[:PAGES_PER_STEP]
    vp_refs = refs[PAGES_PER_STEP:2 * PAGES_PER_STEP]
    o_ref = refs[2 * PAGES_PER_STEP]
    lt_s, po_s, rden_s, acc_s = refs[2 * PAGES_PER_STEP + 1:]
    ph = pl.program_id(1)
    s = pl.program_id(2)
    n_steps = pl.num_programs(2)
    past_len = n_pages * PAGE
    n_blocks = past_len // BLOCK
    ppb = BLOCK // PAGE
    scale = HD_B ** -0.5

    def head_rows(page_ref, h):
        return page_ref[0, pl.ds(h, PAGE, stride=N_HEADS), :]

    def pad_page(new_ref, h):
        blk = new_ref[:, h * HD_B:(h + 1) * HD_B]
        return jnp.concatenate([blk, jnp.zeros((PAGE - dec_seq, HD_B), F32)], axis=0).astype(BF16)

    def q_head(h):
        return q_ref[:, h * HD_B:(h + 1) * HD_B].astype(BF16)

    @pl.when(ph == 0)
    def _():
        for i in range(PAGES_PER_STEP):
            pg = s * PAGES_PER_STEP + i
            for h in range(N_HEADS):
                k_t = head_rows(kp_refs[i], h).T.astype(BF16)
                lt_s[h, pg] = _dot(q_head(h), k_t)

    @pl.when(jnp.logical_and(ph == 0, s == n_steps - 1))
    def _():
        lane = lax.broadcasted_iota(jnp.int32, (dec_seq, LANES), 1)
        rows = []
        for h in range(N_HEADS):
            blk_sum = jnp.sum(lt_s[h].reshape(n_blocks, ppb, dec_seq, PAGE), axis=1)
            tot = jnp.sum(blk_sum, axis=2, keepdims=True) * (1.0 / BLOCK)
            sc_h = jnp.full((dec_seq, LANES), NEG_INF, F32)
            for n in range(n_blocks):
                sc_h = jnp.where(lane == n, tot[n], sc_h)
            rows.append(sc_h)
        sc = jnp.concatenate(rows, axis=0)
        lane_a = lax.broadcasted_iota(jnp.int32, sc.shape, 1)
        rank = jnp.zeros(sc.shape, F32)
        for r in range(1, n_blocks):
            lo = pltpu.roll(sc, r, 1)
            rank += jnp.where(lane_a >= r, jnp.where(lo >= sc, 1.0, 0.0), 0.0)
            hi = pltpu.roll(sc, LANES - r, 1)
            rank += jnp.where(lane_a + r < n_blocks, jnp.where(hi > sc, 1.0, 0.0), 0.0)
        sel_bias = jnp.where(jnp.logical_and(rank < TOPK, lane_a < n_blocks), 0.0, NEG_INF)
        kpos = (lax.broadcasted_iota(jnp.int32, (n_pages, 1, PAGE), 0) * PAGE
                + lax.broadcasted_iota(jnp.int32, (n_pages, 1, PAGE), 2) - past_len).astype(F32)
        qi = lax.broadcasted_iota(jnp.int32, (dec_seq, PAGE), 0)
        ki = lax.broadcasted_iota(jnp.int32, (dec_seq, PAGE), 1)
        for h in range(N_HEADS):
            slope = 2.0 ** -(h + 1)
            bias_h = sel_bias[h * dec_seq:(h + 1) * dec_seq, :]
            bias_pages = jnp.stack([jnp.broadcast_to(bias_h[:, n:n + 1], (dec_seq, PAGE))
                                    for n in range(n_blocks) for _ in range(ppb)], axis=0)
            lgs = lt_s[h] * scale + slope * kpos + bias_pages
            lgo = _dot_nt(q_head(h), pad_page(kn_ref, h))
            lgo = jnp.where(ki <= qi, lgo * scale + slope * ki.astype(F32), NEG_INF)
            mx = jnp.maximum(jnp.max(jnp.max(lgs, axis=0), axis=1, keepdims=True),
                             jnp.max(lgo, axis=1, keepdims=True))
            pr = jnp.exp(lgs - mx[None])
            po = jnp.exp(lgo - mx)
            den = jnp.sum(jnp.sum(pr, axis=0), axis=1, keepdims=True) + jnp.sum(po, axis=1, keepdims=True)
            lt_s[h] = pr
            po_s[h] = po
            rden_s[h] = jnp.broadcast_to(1.0 / den, (dec_seq, LANES))

    @pl.when(jnp.logical_and(ph == 1, s == 0))
    def _():
        for h in range(N_HEADS):
            acc_s[:, h * HD_B:(h + 1) * HD_B] = _dot(po_s[h].astype(BF16), pad_page(vn_ref, h))

    @pl.when(ph == 1)
    def _():
        for i in range(PAGES_PER_STEP):
            pg = s * PAGES_PER_STEP + i
            for h in range(N_HEADS):
                acc_s[:, h * HD_B:(h + 1) * HD_B] += _dot(lt_s[h, pg].astype(BF16),
                                                          head_rows(vp_refs[i], h).astype(BF16))

    @pl.when(jnp.logical_and(ph == 1, s == n_steps - 1))
    def _():
        for h in range(N_HEADS):
            cs = slice(h * HD_B, (h + 1) * HD_B)
            o_ref[:, cs] = acc_s[:, cs] * rden_s[h]


def _moba_sample(page_table, q, k_new, v_new, cache_k, cache_v, *, batch, dec_seq):
    n_pages = page_table.shape[1]
    n_steps = n_pages // PAGES_PER_STEP
    rspec = pl.BlockSpec((dec_seq, D_MODEL), lambda b, ph, s, pt: (b, 0))

    def kmap(i):
        return lambda b, ph, s, pt: (pt[b, jnp.where(ph == 0, s, n_steps - 1) * PAGES_PER_STEP + i], 0, 0)

    def vmap_(i):
        return lambda b, ph, s, pt: (pt[b, jnp.where(ph == 0, 0, s) * PAGES_PER_STEP + i], 0, 0)

    pspec = lambda m: pl.BlockSpec((1, PAGE * N_HEADS, HD_B), m)
    grid_spec = pltpu.PrefetchScalarGridSpec(
        num_scalar_prefetch=1,
        grid=(batch, 2, n_steps),
        in_specs=[rspec, rspec, rspec]
                 + [pspec(kmap(i)) for i in range(PAGES_PER_STEP)]
                 + [pspec(vmap_(i)) for i in range(PAGES_PER_STEP)],
        out_specs=rspec,
        scratch_shapes=[pltpu.VMEM((N_HEADS, n_pages, dec_seq, PAGE), F32),
                        pltpu.VMEM((N_HEADS, dec_seq, PAGE), F32),
                        pltpu.VMEM((N_HEADS, dec_seq, LANES), F32),
                        pltpu.VMEM((dec_seq, D_MODEL), F32)])
    kern = functools.partial(_moba_sample_kernel, n_pages=n_pages, dec_seq=dec_seq)
    return pl.pallas_call(
        kern,
        grid_spec=grid_spec,
        out_shape=jax.ShapeDtypeStruct((batch * dec_seq, D_MODEL), F32),
        compiler_params=_cparams(3),
        name="moba_sample_attn",
    )(page_table, q, k_new, v_new, *([cache_k] * PAGES_PER_STEP), *([cache_v] * PAGES_PER_STEP))


def kernel(x_prompt, x_sample, state_C, state_n, state_m, cache_k, cache_v, page_table,
           w_in_a, b_gate_a, w_out_a, w_q_b, w_out_b, w_kv, w_up, w_down, ln_g, ln_b):
    bp, sp, _ = x_prompt.shape
    bs, ss, _ = x_sample.shape
    ch = MLSTM_CHUNK
    nq = 2 * QK_W + 2 * V_W

    wm_a = w_in_a[0, :, :nq].astype(BF16)
    wg_a = jnp.pad(w_in_a[0, :, nq:], ((0, 0), (0, LANES - 2 * N_HEADS)))
    bg_a = jnp.pad(b_gate_a[0], (0, LANES - 2 * N_HEADS)).reshape(1, LANES)
    wout_a = w_out_a[0].astype(BF16)
    w_qkv_b = jnp.concatenate([w_q_b[0], w_kv], axis=1).astype(BF16)
    wout_b = w_out_b[0].astype(BF16)
    wup = w_up.astype(BF16)
    wdn = w_down.astype(BF16)
    lng = ln_g.reshape(DEPTH, 2, 1, D_MODEL)
    lnb = ln_b.reshape(DEPTH, 2, 1, D_MODEL)

    xp = x_prompt.reshape(bp * sp, D_MODEL)
    qv, kt, o, gi, gf, gt = _proj_a(xp, wm_a, wg_a, bg_a, 512, ch)
    c0 = jnp.zeros((bp, N_HEADS, DK_A, DV_A), F32)
    n0 = jnp.zeros((bp, N_HEADS, DK_A), F32)
    m0 = jnp.zeros((bp, 1, LANES), F32)
    x1, c_p, n_p, m_p = _mlstm(qv, kt, o, gi, gf, gt, xp, c0, n0, m0, wout_a, lng[0, 0], lnb[0, 0],
                             batch=bp, seq=sp, tc=512, ch=ch)
    x2 = _mlp(x1, wup[0], wdn[0], lng[0, 1], lnb[0, 1], 1024, 1024)
    q, k, v, kt, vb, means = _qkv_b(x2, w_qkv_b, 512)
    x3 = _moba_prompt(q, kt, vb, means.reshape(bp, sp // BLOCK, D_MODEL), x2, wout_b, lng[1, 0], lnb[1, 0],
                      batch=bp, seq=sp)
    y_prompt = _mlp(x3, wup[1], wdn[1], lng[1, 1], lnb[1, 1], 1024, 1024)

    rows_s = bs * ss
    xs = x_sample.reshape(rows_s, D_MODEL)
    qv_s, kt_s, o_s, gi_s, gf_s, _ = _proj_a(xs, wm_a, wg_a, bg_a, rows_s, ch)
    pad3 = lambda a, val=0.0: jnp.pad(a.reshape(bs, ss, -1), ((0, 0), (0, ch - ss), (0, 0)),
                                      constant_values=val)
    flat = lambda a: a.reshape(bs * ch, -1)
    gi_pad = jnp.where(jnp.arange(LANES)[None, None, :] < N_HEADS, pad3(gi_s, NEG_INF), 0.0)
    gf_pad = pad3(gf_s)
    gt_pad = jnp.concatenate([jnp.swapaxes(gi_pad, 1, 2)[:, :N_HEADS],
                              jnp.swapaxes(gf_pad, 1, 2)[:, :N_HEADS]], axis=1)
    m0s = jnp.pad(state_m[0], ((0, 0), (0, LANES - N_HEADS))).reshape(bs, 1, LANES)
    kt_pad = jnp.pad(kt_s.reshape(QK_W, bs, ss), ((0, 0), (0, 0), (0, ch - ss))).reshape(QK_W, bs * ch)
    x1s_pad, c_s, n_s, m_s = _mlstm(flat(pad3(qv_s)), kt_pad, flat(pad3(o_s)), flat(gi_pad), flat(gf_pad), gt_pad,
                                    flat(pad3(xs)), state_C[0], state_n[0], m0s, wout_a, lng[0, 0], lnb[0, 0],
                                  batch=bs, seq=ch, tc=ch, ch=ch)
    x1s = x1s_pad.reshape(bs, ch, D_MODEL)[:, :ss].reshape(rows_s, D_MODEL)
    x2s = _mlp(x1s, wup[0], wdn[0], lng[0, 1], lnb[0, 1], rows_s, 512)
    q_s, k_s, v_s, _, _, _ = _qkv_b(x2s, w_qkv_b, rows_s)
    n_pool = cache_k.shape[0]
    att_s = _moba_sample(page_table, q_s.astype(F32), k_s, v_s,
                         cache_k.reshape(n_pool, PAGE * N_HEADS, HD_B),
                         cache_v.reshape(n_pool, PAGE * N_HEADS, HD_B), batch=bs, dec_seq=ss)
    x3s = _out_ln(att_s, wout_b, x2s, lng[1, 0], lnb[1, 0], rows_s)
    y_sample = _mlp(x3s, wup[1], wdn[1], lng[1, 1], lnb[1, 1], rows_s, 512)

    p_c, p_n, p_m = c_p[None], n_p[None], m_p[None, :, 0, :N_HEADS]
    s_c, s_n, s_m = c_s[None], n_s[None], m_s[None, :, 0, :N_HEADS]
    return (y_prompt.reshape(bp, sp, D_MODEL), y_sample.reshape(bs, ss, D_MODEL),
            p_c, p_n, p_m,
            k.reshape(bp, sp, N_HEADS, HD_B), v.reshape(bp, sp, N_HEADS, HD_B),
            s_c, s_n, s_m,
            k_s.reshape(bs, ss, N_HEADS, HD_B), v_s.reshape(bs, ss, N_HEADS, HD_B))
```

```python
import functools

import numpy as np
import jax
import jax.numpy as jnp
from jax import lax
from jax.experimental import pallas as pl
from jax.experimental.pallas import tpu as pltpu

F32 = jnp.float32
BF16 = jnp.bfloat16

D_MODEL = 1024
N_HEADS = 8
DK_A = 64
DV_A = 128
QK_W = N_HEADS * DK_A
V_W = N_HEADS * DV_A
HD_B = 128
BLOCK = 256
TOPK = 3
PAGE = 128
D_FF = 4 * D_MODEL
DEPTH = 2
ALPHA = (2.0 * DEPTH) ** 0.25
LN_EPS = 1e-5
NEG_INF = float("-inf")

LANES = 128
VMEM_LIMIT = 56 * 1024 * 1024

MLSTM_CHUNK = 128
M_FLOOR = -1e30
LOG2E = 1.4426950408889634


def _cparams(n_axes, flags=None):
    return pltpu.CompilerParams(dimension_semantics=("arbitrary",) * n_axes,
                                vmem_limit_bytes=VMEM_LIMIT, flags=flags)


def _dot(a, b):
    return jnp.dot(a, b, preferred_element_type=F32)


def _dot_nt(a, b):
    return lax.dot_general(a, b, (((1,), (1,)), ((), ())), preferred_element_type=F32)


def _dot_tn(a, b):
    return lax.dot_general(a, b, (((0,), (0,)), ((), ())), preferred_element_type=F32)


def _split3(x):
    x1 = x.astype(BF16)
    r1 = x - x1.astype(F32)
    x2 = r1.astype(BF16)
    x3 = (r1 - x2.astype(F32)).astype(BF16)
    return x1, x2, x3


def _layer_norm(y, g, b):
    mu = jnp.mean(y, axis=-1, keepdims=True)
    d = y - mu
    var = jnp.mean(d * d, axis=-1, keepdims=True)
    return d * lax.rsqrt(var + LN_EPS) * g + b


def _cummax_rows(x):
    row = lax.broadcasted_iota(jnp.int32, x.shape, 0)
    shift = 1
    while shift < x.shape[0]:
        x = jnp.maximum(x, jnp.where(row >= shift, pltpu.roll(x, shift, 0), NEG_INF))
        shift *= 2
    return x


def _proj_a_kernel(x_ref, wm_ref, wg_ref, bg_ref, qv_ref, kt_ref, o_ref, gi_ref, gf_ref, gt_ref, *, ch):
    x = x_ref[...]
    xh = x.astype(BF16)
    nq = 2 * QK_W + V_W
    step = 512
    for c0 in range(0, nq, step):
        acc = _dot(xh, wm_ref[:, c0:c0 + step])
        if c0 == QK_W:
            kt_ref[...] = (acc * (DK_A ** -0.5)).T.astype(BF16)
        else:
            dst = c0 if c0 < QK_W else c0 - QK_W
            qv_ref[:, dst:dst + step] = acc.astype(BF16)
    for c0 in range(0, V_W, step):
        o_ref[:, c0:c0 + step] = _dot(xh, wm_ref[:, nq + c0:nq + c0 + step])
    xl = (x - xh.astype(F32)).astype(BF16)
    wg = wg_ref[...]
    wgh = wg.astype(BF16)
    wgl = (wg - wgh.astype(F32)).astype(BF16)
    g = _dot(xh, wgh) + _dot(xl, wgh) + _dot(xh, wgl) + bg_ref[...]
    lane = lax.broadcasted_iota(jnp.int32, g.shape, 1)
    log_f = jnp.minimum(g, 0.0) - jnp.log1p(jnp.exp(-jnp.abs(g)))
    gi_ref[...] = jnp.where(lane < N_HEADS, g, 0.0)
    gf_ref[...] = jnp.where(lane < N_HEADS, pltpu.roll(log_f, LANES - N_HEADS, 1), 0.0)
    both_t = jnp.where(lane < N_HEADS, g, jnp.where(lane < 2 * N_HEADS, log_f, 0.0)).T
    for c in range(x.shape[0] // ch):
        gt_ref[c] = both_t[0:2 * N_HEADS, c * ch:(c + 1) * ch]


def _proj_a(x, wm, wg, bg, tm, ch):
    rows = x.shape[0]
    return pl.pallas_call(
        functools.partial(_proj_a_kernel, ch=ch),
        grid=(rows // tm,),
        in_specs=[pl.BlockSpec((tm, D_MODEL), lambda i: (i, 0)),
                  pl.BlockSpec(wm.shape, lambda i: (0, 0)),
                  pl.BlockSpec(wg.shape, lambda i: (0, 0)),
                  pl.BlockSpec(bg.shape, lambda i: (0, 0))],
        out_specs=[pl.BlockSpec((tm, QK_W + V_W), lambda i: (i, 0)),
                   pl.BlockSpec((QK_W, tm), lambda i: (0, i)),
                   pl.BlockSpec((tm, V_W), lambda i: (i, 0)),
                   pl.BlockSpec((tm, LANES), lambda i: (i, 0)),
                   pl.BlockSpec((tm, LANES), lambda i: (i, 0)),
                   pl.BlockSpec((tm // ch, 2 * N_HEADS, ch), lambda i: (i, 0, 0))],
        out_shape=[jax.ShapeDtypeStruct((rows, QK_W + V_W), BF16),
                   jax.ShapeDtypeStruct((QK_W, rows), BF16),
                   jax.ShapeDtypeStruct((rows, V_W), F32),
                   jax.ShapeDtypeStruct((rows, LANES), F32),
                   jax.ShapeDtypeStruct((rows, LANES), F32),
                   jax.ShapeDtypeStruct((rows // ch, 2 * N_HEADS, ch), F32)],
        compiler_params=_cparams(1),
        name="mlstm_in_proj",
    )(x, wm, wg, bg)


def _mlstm_kernel(qv_ref, kt_ref, o_ref, gi_ref, gf_ref, gt_ref, x_ref, c0_ref, n0_ref, m0_ref, wout_ref,
                  lng_ref, lnb_ref, y_ref, ct_ref, nt_ref, mt_ref, caug_s, m_s, h_s, *, tc, ch, valid):
    t = pl.program_id(1)

    def slab(ref, r0, c0, c1, dtype):
        if valid == ch:
            return ref[r0:r0 + ch, c0:c1]
        return jnp.concatenate([ref[:, c0:c1], jnp.zeros((ch - valid, c1 - c0), F32)], axis=0).astype(dtype)

    eye_k = (lax.broadcasted_iota(jnp.int32, (DK_A, DK_A), 0)
             == lax.broadcasted_iota(jnp.int32, (DK_A, DK_A), 1))

    @pl.when(t == 0)
    def _():
        for h in range(N_HEADS):
            caug_s[h, :, :DV_A] = c0_ref[0, h]
            n_row = jnp.broadcast_to(n0_ref[0, h:h + 1, :], (DK_A, DK_A))
            n_col = jnp.sum(jnp.where(eye_k, n_row, 0.0), axis=1, keepdims=True)
            caug_s[h, :, DV_A:] = jnp.broadcast_to(n_col, (DK_A, DV_A))
        m_s[...] = m0_ref[0]

    row = lax.broadcasted_iota(jnp.int32, (ch, ch), 0)
    col = lax.broadcasted_iota(jnp.int32, (ch, ch), 1)
    tril = row >= col
    tri_b = jnp.where(tril, 1.0, 0.0).astype(BF16)
    tri_t = jnp.where(row <= col, 1.0, 0.0).astype(BF16)
    ones_blk = jnp.ones((ch, LANES), BF16)
    trow = lax.broadcasted_iota(jnp.int32, (2 * N_HEADS, ch), 0)

    for j in range(tc // ch):
        r0 = j * ch
        gi = gi_ref[r0:r0 + ch, :]
        f1, f2, f3 = _split3(gf_ref[r0:r0 + ch, :])
        b = _dot(tri_b, f1) + _dot(tri_b, f2) + _dot(tri_b, f3)
        u = gi - b
        m_prev = m_s[...]
        big_m = jnp.maximum(_cummax_rows(u), m_prev)
        a_all = jnp.exp(m_prev - big_m)
        e_all = jnp.exp(-(b + big_m))
        m_end = big_m[ch - 1:ch, :]
        ae_all = a_all[ch - 1:ch, :]
        gt = gt_ref[j]
        r1, r2, r3 = _split3(jnp.where(trow >= N_HEADS, gt, 0.0))
        b_t = _dot(r1, tri_t) + _dot(r2, tri_t) + _dot(r3, tri_t)
        u_t = gt[0:N_HEADS, :] - b_t[N_HEADS:2 * N_HEADS, :]
        for h in range(N_HEADS):
            w = jnp.exp(jnp.where(tril, u_t[h:h + 1, :] - big_m[:, h:h + 1], NEG_INF))
            q = slab(qv_ref, r0, h * DK_A, (h + 1) * DK_A, BF16)
            k_t = kt_ref[h * DK_A:(h + 1) * DK_A, r0:r0 + ch]
            v = slab(qv_ref, r0, QK_W + h * DV_A, QK_W + (h + 1) * DV_A, BF16)
            vaug = jnp.concatenate([v, ones_blk], axis=1)
            sw = _dot(q, k_t) * w
            qc = _dot(q, caug_s[h].astype(BF16))
            a = jnp.broadcast_to(a_all[:, h:h + 1], (ch, LANES))
            tot = _dot(sw.astype(BF16), vaug) + jnp.concatenate([a, a], axis=1) * qc
            floor = jnp.broadcast_to(e_all[:, h:h + 1], (ch, LANES))
            rden = 1.0 / jnp.maximum(jnp.abs(tot[:, DV_A:]), floor)
            og = slab(o_ref, r0, h * DV_A, (h + 1) * DV_A, F32)
            hh = tot[:, :DV_A] * rden * (1.0 / (1.0 + jnp.exp(-og)))
            h_s[r0:r0 + ch, h * DV_A:(h + 1) * DV_A] = hh.astype(BF16)
            ws_t = jnp.exp(u_t[h:h + 1, :] - m_end[:, h:h + 1])
            kw_t = (k_t.astype(F32) * ws_t).astype(BF16)
            caug_s[h] = ae_all[:, h:h + 1] * caug_s[h] + _dot(kw_t, vaug)
        m_s[...] = b[ch - 1:ch, :] + m_end

    if valid == ch:
        y = _dot(h_s[...], wout_ref[...])
    else:
        y = _dot(h_s[0:16, :], wout_ref[...])[0:valid]
    y_ref[...] = _layer_norm(ALPHA * x_ref[...] + y, lng_ref[...], lnb_ref[...])

    @pl.when(t == pl.num_programs(1) - 1)
    def _():
        for h in range(N_HEADS):
            ct_ref[0, h] = caug_s[h, :, :DV_A]
            n_col = jnp.broadcast_to(caug_s[h, :, DV_A:DV_A + 1], (DK_A, DK_A))
            nt_ref[0, h:h + 1, :] = jnp.sum(jnp.where(eye_k, n_col, 0.0), axis=0, keepdims=True)
        mt_ref[0] = m_s[...]


def _mlstm(qv, kt, o, gi, gf, gt, x, c0, n0, m0, wout, lng, lnb, *, batch, seq, tc, ch, valid=None):
    n_t = seq // tc
    valid = ch if valid is None else valid
    tr = tc if valid == ch else valid
    kern = functools.partial(_mlstm_kernel, tc=tc, ch=ch, valid=valid)
    rmap = lambda b, t: (b * n_t + t, 0)
    cmap = lambda b, t: (0, 0)
    smap = lambda b, t: (b, 0, 0, 0)
    mmap = lambda b, t: (b, 0, 0)
    return pl.pallas_call(
        kern,
        grid=(batch, n_t),
        in_specs=[pl.BlockSpec((tr, QK_W + V_W), rmap),
                  pl.BlockSpec((QK_W, tc), lambda b, t: (0, b * n_t + t)),
                  pl.BlockSpec((tr, V_W), rmap),
                  pl.BlockSpec((tc, LANES), rmap),
                  pl.BlockSpec((tc, LANES), rmap),
                  pl.BlockSpec((tc // ch, 2 * N_HEADS, ch), lambda b, t: (b * n_t + t, 0, 0)),
                  pl.BlockSpec((tr, D_MODEL), rmap),
                  pl.BlockSpec((1, N_HEADS, DK_A, DV_A), smap),
                  pl.BlockSpec((1, N_HEADS, DK_A), mmap),
                  pl.BlockSpec((1, 1, LANES), mmap),
                  pl.BlockSpec(wout.shape, cmap),
                  pl.BlockSpec(lng.shape, cmap),
                  pl.BlockSpec(lnb.shape, cmap)],
        out_specs=[pl.BlockSpec((tr, D_MODEL), rmap),
                   pl.BlockSpec((1, N_HEADS, DK_A, DV_A), smap),
                   pl.BlockSpec((1, N_HEADS, DK_A), mmap),
                   pl.BlockSpec((1, 1, LANES), mmap)],
        out_shape=[jax.ShapeDtypeStruct((batch * n_t * tr, D_MODEL), F32),
                   jax.ShapeDtypeStruct((batch, N_HEADS, DK_A, DV_A), F32),
                   jax.ShapeDtypeStruct((batch, N_HEADS, DK_A), F32),
                   jax.ShapeDtypeStruct((batch, 1, LANES), F32)],
        scratch_shapes=[pltpu.VMEM((N_HEADS, DK_A, 2 * DV_A), F32),
                        pltpu.VMEM((1, LANES), F32),
                        pltpu.VMEM((tc, V_W), BF16)],
        compiler_params=_cparams(2),
        name="mlstm_scan",
    )(qv, kt, o, gi, gf, gt, x, c0, n0, m0, wout, lng, lnb)


def _mlp_kernel(x_ref, wup_ref, wdn_ref, lng_ref, lnb_ref, y_ref, acc_s, xb_s):
    j = pl.program_id(1)

    @pl.when(j == 0)
    def _():
        xb_s[...] = x_ref[...].astype(BF16)
        acc_s[...] = jnp.zeros_like(acc_s)

    hid = jnp.maximum(_dot(xb_s[...], wup_ref[...]), 0.0)
    acc_s[...] += _dot((hid * hid).astype(BF16), wdn_ref[...])

    @pl.when(j == pl.num_programs(1) - 1)
    def _():
        y_ref[...] = _layer_norm(ALPHA * x_ref[...] + acc_s[...], lng_ref[...], lnb_ref[...])


def _mlp(x, wup, wdn, lng, lnb, tm, tf, name="mlp_ln"):
    rows = x.shape[0]
    return pl.pallas_call(
        _mlp_kernel,
        grid=(rows // tm, D_FF // tf),
        in_specs=[pl.BlockSpec((tm, D_MODEL), lambda i, j: (i, 0)),
                  pl.BlockSpec((D_MODEL, tf), lambda i, j: (0, j)),
                  pl.BlockSpec((tf, D_MODEL), lambda i, j: (j, 0)),
                  pl.BlockSpec(lng.shape, lambda i, j: (0, 0)),
                  pl.BlockSpec(lnb.shape, lambda i, j: (0, 0))],
        out_specs=pl.BlockSpec((tm, D_MODEL), lambda i, j: (i, 0)),
        out_shape=jax.ShapeDtypeStruct((rows, D_MODEL), F32),
        scratch_shapes=[pltpu.VMEM((tm, D_MODEL), F32), pltpu.VMEM((tm, D_MODEL), BF16)],
        compiler_params=_cparams(2),
        name=name,
    )(x, wup, wdn, lng, lnb)


def _qkv_b_kernel(x_ref, w_ref, q_ref, k_ref, v_ref, kt_ref, vb_ref, mean_ref, *, tm):
    xh = x_ref[...].astype(BF16)
    step = 512
    for c0 in range(0, D_MODEL, step):
        q_ref[:, c0:c0 + step] = _dot(xh, w_ref[:, c0:c0 + step]).astype(BF16)
        kk = _dot(xh, w_ref[:, D_MODEL + c0:D_MODEL + c0 + step])
        k_ref[:, c0:c0 + step] = kk
        for hh in range(step // HD_B):
            for r in range(tm // BLOCK):
                kt_ref[c0 // HD_B + hh, r] = kk[r * BLOCK:(r + 1) * BLOCK, hh * HD_B:(hh + 1) * HD_B].T.astype(BF16)
        for r in range(tm // BLOCK):
            mean_ref[r, :, c0:c0 + step] = jnp.mean(kk[r * BLOCK:(r + 1) * BLOCK], axis=0, keepdims=True)
        vv = _dot(xh, w_ref[:, 2 * D_MODEL + c0:2 * D_MODEL + c0 + step])
        v_ref[:, c0:c0 + step] = vv
        vb_ref[:, c0:c0 + step] = vv.astype(BF16)


def _qkv_b(x, w, tm):
    rows = x.shape[0]
    rmap = lambda i: (i, 0)
    rspec = pl.BlockSpec((tm, D_MODEL), rmap)
    return pl.pallas_call(
        functools.partial(_qkv_b_kernel, tm=tm),
        grid=(rows // tm,),
        in_specs=[rspec, pl.BlockSpec(w.shape, lambda i: (0, 0))],
        out_specs=[rspec, rspec, rspec,
                   pl.BlockSpec((N_HEADS, tm // BLOCK, HD_B, BLOCK), lambda i: (0, i, 0, 0)),
                   rspec,
                   pl.BlockSpec((tm // BLOCK, 1, D_MODEL), lambda i: (i, 0, 0))],
        out_shape=[jax.ShapeDtypeStruct((rows, D_MODEL), BF16),
                   jax.ShapeDtypeStruct((rows, D_MODEL), F32),
                   jax.ShapeDtypeStruct((rows, D_MODEL), F32),
                   jax.ShapeDtypeStruct((N_HEADS, rows // BLOCK, HD_B, BLOCK), BF16),
                   jax.ShapeDtypeStruct((rows, D_MODEL), BF16),
                   jax.ShapeDtypeStruct((rows // BLOCK, 1, D_MODEL), F32)],
        compiler_params=_cparams(1),
        name="moba_qkv_proj",
    )(x, w)


def _out_ln_kernel(a_ref, w_ref, x_ref, lng_ref, lnb_ref, y_ref):
    y = _dot(a_ref[...].astype(BF16), w_ref[...])
    y_ref[...] = _layer_norm(ALPHA * x_ref[...] + y, lng_ref[...], lnb_ref[...])


def _out_ln(a, w, x, lng, lnb, tm):
    rows = x.shape[0]
    rspec = pl.BlockSpec((tm, D_MODEL), lambda i: (i, 0))
    cmap = lambda i: (0, 0)
    return pl.pallas_call(
        _out_ln_kernel,
        grid=(rows // tm,),
        in_specs=[rspec, pl.BlockSpec(w.shape, cmap), rspec,
                  pl.BlockSpec(lng.shape, cmap), pl.BlockSpec(lnb.shape, cmap)],
        out_specs=rspec,
        out_shape=jax.ShapeDtypeStruct((rows, D_MODEL), F32),
        compiler_params=_cparams(1),
        name="attn_out_ln",
    )(a, w, x, lng, lnb)


def _head_slopes(rep):
    slopes = np.zeros((1, LANES), np.float32)
    slopes[0, :N_HEADS * rep] = np.repeat(2.0 ** -np.arange(1, N_HEADS + 1, dtype=np.float64), rep)
    return jnp.asarray(slopes)


def _head_block_diag(rows_8):
    tiled = jnp.concatenate([rows_8] * (LANES // 8), axis=0)
    r = lax.broadcasted_iota(jnp.int32, (LANES, D_MODEL), 0)
    c = lax.broadcasted_iota(jnp.int32, (LANES, D_MODEL), 1)
    keep = (r // 8) == (c // HD_B)
    return jnp.where(keep, tiled, 0.0).astype(BF16)


def _moba_prompt_kernel(q_ref, kt_ref, vb_ref, mean_ref, slope_ref, x_ref, wout_ref, lng_ref, lnb_ref, o_ref,
                        sel_s, mask_s, m_s, l_s, acc_s, s_s):
    qi = pl.program_id(1)
    scale = HD_B ** -0.5
    rows = BLOCK

    @pl.when(jnp.logical_and(pl.program_id(0) == 0, qi == 0))
    def _():
        t_i = lax.broadcasted_iota(jnp.int32, (BLOCK, BLOCK), 0)
        j_i = lax.broadcasted_iota(jnp.int32, (BLOCK, BLOCK), 1)
        jf = j_i.astype(F32)
        for h in range(N_HEADS):
            mask_s[0, h] = (LOG2E * 2.0 ** -(h + 1)) * jf
            mask_s[1, h] = jnp.where(j_i > t_i, NEG_INF, (LOG2E * 2.0 ** -(h + 1)) * jf)

    mbd = _head_block_diag(mean_ref[0])
    sc = _dot_nt(q_ref[...], mbd)
    lane = lax.broadcasted_iota(jnp.int32, sc.shape, 1)
    blk = lane % 8
    past = jnp.logical_and(lane < N_HEADS * 8, blk < qi)
    sm = jnp.where(past, sc, NEG_INF)
    rank = jnp.zeros(sc.shape, F32)
    for r in range(1, 8):
        lo = pltpu.roll(sm, r, 1)
        rank += jnp.where(blk >= r, jnp.where(lo >= sm, 1.0, 0.0), 0.0)
        hi = pltpu.roll(sm, LANES - r, 1)
        rank += jnp.where(blk + r < 8, jnp.where(hi > sm, 1.0, 0.0), 0.0)
    sel_s[...] = jnp.where(past, jnp.where(rank < TOPK, 1.0, 0.0), 0.0).astype(BF16)
    m_s[...] = jnp.full(m_s.shape, M_FLOOR, F32)
    l_s[...] = jnp.zeros_like(l_s)
    acc_s[...] = jnp.zeros_like(acc_s)

    er = lax.broadcasted_iota(jnp.int32, (LANES, LANES), 0)
    ec = lax.broadcasted_iota(jnp.int32, (LANES, LANES), 1)
    ones_v = jnp.ones((BLOCK, HD_B), BF16)

    def key_block(kn, carry):
        pick = jnp.where(jnp.logical_and(er == ec * 8 + kn, ec < N_HEADS), 1.0, 0.0).astype(BF16)
        selcols = _dot(sel_s[...], pick)
        kbase = slope_ref[...] * (LOG2E * (kn * BLOCK).astype(F32))
        own = kn == qi
        biasc = jnp.where(jnp.logical_or(selcols > 0.5, own), kbase, NEG_INF)
        own_i = own.astype(jnp.int32)
        r0 = pl.multiple_of(kn * BLOCK, BLOCK)

        for h in range(N_HEADS):
            cs = slice(h * HD_B, (h + 1) * HD_B)
            s_s[h] = _dot(q_ref[:, cs], kt_ref[h, kn])

        for h in range(N_HEADS):
            cs = slice(h * HD_B, (h + 1) * HD_B)
            vaug = jnp.concatenate([vb_ref[pl.ds(r0, BLOCK), cs], ones_v], axis=1)
            bias = jnp.broadcast_to(biasc[:, h:h + 1], (rows, LANES))
            lg = s_s[h] * (scale * LOG2E) + mask_s[own_i, h] + jnp.concatenate([bias, bias], axis=1)
            m_old = m_s[h]
            m_new = jnp.maximum(m_old, jnp.max(lg, axis=1, keepdims=True))
            alpha = jnp.exp2(m_old - m_new)
            pr = jnp.exp2(lg - jnp.concatenate([m_new, m_new], axis=1))
            pv = _dot(pr.astype(BF16), vaug)
            l_s[h] = alpha * l_s[h] + pv[:, HD_B:]
            acc_s[:, cs] = alpha * acc_s[:, cs] + pv[:, :HD_B]
            m_s[h] = m_new
        return carry

    lax.fori_loop(0, qi + 1, key_block, 0)

    for h in range(N_HEADS):
        cs = slice(h * HD_B, (h + 1) * HD_B)
        acc_s[:, cs] = acc_s[:, cs] / l_s[h]
    y = _dot(acc_s[...].astype(BF16), wout_ref[...])
    o_ref[...] = _layer_norm(ALPHA * x_ref[...] + y, lng_ref[...], lnb_ref[...])


def _moba_prompt(q, kt, vb, means, x, wout, lng, lnb, *, batch, seq):
    nb = seq // BLOCK
    qmap = lambda b, i: (b * nb + i, 0)
    cmap = lambda b, i: (0, 0)
    return pl.pallas_call(
        _moba_prompt_kernel,
        grid=(batch, nb),
        in_specs=[pl.BlockSpec((BLOCK, D_MODEL), qmap),
                  pl.BlockSpec((N_HEADS, nb, HD_B, BLOCK), lambda b, i: (0, b, 0, 0)),
                  pl.BlockSpec((seq, D_MODEL), lambda b, i: (b, 0)),
                  pl.BlockSpec((1, nb, D_MODEL), lambda b, i: (b, 0, 0)),
                  pl.BlockSpec((1, LANES), cmap),
                  pl.BlockSpec((BLOCK, D_MODEL), qmap),
                  pl.BlockSpec(wout.shape, cmap),
                  pl.BlockSpec(lng.shape, cmap),
                  pl.BlockSpec(lnb.shape, cmap)],
        out_specs=pl.BlockSpec((BLOCK, D_MODEL), qmap),
        out_shape=jax.ShapeDtypeStruct((batch * seq, D_MODEL), F32),
        scratch_shapes=[pltpu.VMEM((BLOCK, LANES), BF16),
                        pltpu.VMEM((2, N_HEADS, BLOCK, BLOCK), F32),
                        pltpu.VMEM((N_HEADS, BLOCK, LANES), F32),
                        pltpu.VMEM((N_HEADS, BLOCK, LANES), F32),
                        pltpu.VMEM((BLOCK, D_MODEL), F32),
                        pltpu.VMEM((N_HEADS, BLOCK, BLOCK), F32)],
        compiler_params=_cparams(2),
        name="moba_prompt_attn",
    )(q, kt, vb, means, _head_slopes(1), x, wout, lng, lnb)


PAGES_PER_STEP = 16


def _moba_sample_kernel(pt_ref, q_ref, kn_ref, vn_ref, *refs, n_pages, dec_seq):
    kp_refs = refs[:PAGES_PER_STEP]
    vp_refs = refs[PAGES_PER_STEP:2 * PAGES_PER_STEP]
    o_ref = refs[2 * PAGES_PER_STEP]
    lt_s, po_s, rden_s, acc_s = refs[2 * PAGES_PER_STEP + 1:]
    ph = pl.program_id(1)
    s = pl.program_id(2)
    n_steps = pl.num_programs(2)
    past_len = n_pages * PAGE
    n_blocks = past_len // BLOCK
    ppb = BLOCK // PAGE
    scale = HD_B ** -0.5

    def head_rows(page_ref, h):
        return page_ref[0, pl.ds(h, PAGE, stride=N_HEADS), :]

    def pad_page(new_ref, h):
        blk = new_ref[:, h * HD_B:(h + 1) * HD_B]
        return jnp.concatenate([blk, jnp.zeros((PAGE - dec_seq, HD_B), F32)], axis=0).astype(BF16)

    def q_head(h):
        return q_ref[:, h * HD_B:(h + 1) * HD_B].astype(BF16)

    @pl.when(ph == 0)
    def _():
        for i in range(PAGES_PER_STEP):
            pg = s * PAGES_PER_STEP + i
            for h in range(N_HEADS):
                k_t = head_rows(kp_refs[i], h).T.astype(BF16)
                lt_s[h, pg] = _dot(q_head(h), k_t)

    @pl.when(jnp.logical_and(ph == 0, s == n_steps - 1))
    def _():
        lane = lax.broadcasted_iota(jnp.int32, (dec_seq, LANES), 1)
        rows = []
        for h in range(N_HEADS):
            blk_sum = jnp.sum(lt_s[h].reshape(n_blocks, ppb, dec_seq, PAGE), axis=1)
            tot = jnp.sum(blk_sum, axis=2, keepdims=True) * (1.0 / BLOCK)
            sc_h = jnp.full((dec_seq, LANES), NEG_INF, F32)
            for n in range(n_blocks):
                sc_h = jnp.where(lane == n, tot[n], sc_h)
            rows.append(sc_h)
        sc = jnp.concatenate(rows, axis=0)
        lane_a = lax.broadcasted_iota(jnp.int32, sc.shape, 1)
        rank = jnp.zeros(sc.shape, F32)
        for r in range(1, n_blocks):
            lo = pltpu.roll(sc, r, 1)
            rank += jnp.where(lane_a >= r, jnp.where(lo >= sc, 1.0, 0.0), 0.0)
            hi = pltpu.roll(sc, LANES - r, 1)
            rank += jnp.where(lane_a + r < n_blocks, jnp.where(hi > sc, 1.0, 0.0), 0.0)
        sel_bias = jnp.where(jnp.logical_and(rank < TOPK, lane_a < n_blocks), 0.0, NEG_INF)
        kpos = (lax.broadcasted_iota(jnp.int32, (n_pages, 1, PAGE), 0) * PAGE
                + lax.broadcasted_iota(jnp.int32, (n_pages, 1, PAGE), 2) - past_len).astype(F32)
        qi = lax.broadcasted_iota(jnp.int32, (dec_seq, PAGE), 0)
        ki = lax.broadcasted_iota(jnp.int32, (dec_seq, PAGE), 1)
        for h in range(N_HEADS):
            slope = 2.0 ** -(h + 1)
            bias_h = sel_bias[h * dec_seq:(h + 1) * dec_seq, :]
            bias_pages = jnp.stack([jnp.broadcast_to(bias_h[:, n:n + 1], (dec_seq, PAGE))
                                    for n in range(n_blocks) for _ in range(ppb)], axis=0)
            lgs = lt_s[h] * scale + slope * kpos + bias_pages
            lgo = _dot_nt(q_head(h), pad_page(kn_ref, h))
            lgo = jnp.where(ki <= qi, lgo * scale + slope * ki.astype(F32), NEG_INF)
            mx = jnp.maximum(jnp.max(jnp.max(lgs, axis=0), axis=1, keepdims=True),
                             jnp.max(lgo, axis=1, keepdims=True))
            pr = jnp.exp(lgs - mx[None])
            po = jnp.exp(lgo - mx)
            den = jnp.sum(jnp.sum(pr, axis=0), axis=1, keepdims=True) + jnp.sum(po, axis=1, keepdims=True)
            lt_s[h] = pr
            po_s[h] = po
            rden_s[h] = jnp.broadcast_to(1.0 / den, (dec_seq, LANES))

    @pl.when(jnp.logical_and(ph == 1, s == 0))
    def _():
        for h in range(N_HEADS):
            acc_s[:, h * HD_B:(h + 1) * HD_B] = _dot(po_s[h].astype(BF16), pad_page(vn_ref, h))

    @pl.when(ph == 1)
    def _():
        for i in range(PAGES_PER_STEP):
            pg = s * PAGES_PER_STEP + i
            for h in range(N_HEADS):
                acc_s[:, h * HD_B:(h + 1) * HD_B] += _dot(lt_s[h, pg].astype(BF16),
                                                          head_rows(vp_refs[i], h).astype(BF16))

    @pl.when(jnp.logical_and(ph == 1, s == n_steps - 1))
    def _():
        for h in range(N_HEADS):
            cs = slice(h * HD_B, (h + 1) * HD_B)
            o_ref[:, cs] = acc_s[:, cs] * rden_s[h]


def _moba_sample(page_table, q, k_new, v_new, cache_k, cache_v, *, batch, dec_seq):
    n_pages = page_table.shape[1]
    n_steps = n_pages // PAGES_PER_STEP
    rspec = pl.BlockSpec((dec_seq, D_MODEL), lambda b, ph, s, pt: (b, 0))

    def kmap(i):
        return lambda b, ph, s, pt: (pt[b, jnp.where(ph == 0, s, n_steps - 1) * PAGES_PER_STEP + i], 0, 0)

    def vmap_(i):
        return lambda b, ph, s, pt: (pt[jnp.where(ph == 0, jnp.maximum(b - 1, 0), b),
                                        jnp.where(ph == 0, n_steps - 1, s) * PAGES_PER_STEP + i], 0, 0)

    pspec = lambda m: pl.BlockSpec((1, PAGE * N_HEADS, HD_B), m)
    grid_spec = pltpu.PrefetchScalarGridSpec(
        num_scalar_prefetch=1,
        grid=(batch, 2, n_steps),
        in_specs=[rspec, rspec, rspec]
                 + [pspec(kmap(i)) for i in range(PAGES_PER_STEP)]
                 + [pspec(vmap_(i)) for i in range(PAGES_PER_STEP)],
        out_specs=rspec,
        scratch_shapes=[pltpu.VMEM((N_HEADS, n_pages, dec_seq, PAGE), F32),
                        pltpu.VMEM((N_HEADS, dec_seq, PAGE), F32),
                        pltpu.VMEM((N_HEADS, dec_seq, LANES), F32),
                        pltpu.VMEM((dec_seq, D_MODEL), F32)])
    kern = functools.partial(_moba_sample_kernel, n_pages=n_pages, dec_seq=dec_seq)
    return pl.pallas_call(
        kern,
        grid_spec=grid_spec,
        out_shape=jax.ShapeDtypeStruct((batch * dec_seq, D_MODEL), F32),
        compiler_params=_cparams(3),
        name="moba_sample_attn",
    )(page_table, q, k_new, v_new, *([cache_k] * PAGES_PER_STEP), *([cache_v] * PAGES_PER_STEP))


def kernel(x_prompt, x_sample, state_C, state_n, state_m, cache_k, cache_v, page_table,
           w_in_a, b_gate_a, w_out_a, w_q_b, w_out_b, w_kv, w_up, w_down, ln_g, ln_b):
    bp, sp, _ = x_prompt.shape
    bs, ss, _ = x_sample.shape
    ch = MLSTM_CHUNK
    nq = 2 * QK_W + 2 * V_W

    wm_a = w_in_a[0, :, :nq].astype(BF16)
    wg_a = jnp.pad(w_in_a[0, :, nq:], ((0, 0), (0, LANES - 2 * N_HEADS)))
    bg_a = jnp.pad(b_gate_a[0], (0, LANES - 2 * N_HEADS)).reshape(1, LANES)
    wout_a = w_out_a[0].astype(BF16)
    w_qkv_b = jnp.concatenate([w_q_b[0], w_kv], axis=1).astype(BF16)
    wout_b = w_out_b[0].astype(BF16)
    wup = w_up.astype(BF16)
    wdn = w_down.astype(BF16)
    lng = ln_g.reshape(DEPTH, 2, 1, D_MODEL)
    lnb = ln_b.reshape(DEPTH, 2, 1, D_MODEL)

    xp = x_prompt.reshape(bp * sp, D_MODEL)
    qv, kt, o, gi, gf, gt = _proj_a(xp, wm_a, wg_a, bg_a, 512, ch)
    c0 = jnp.zeros((bp, N_HEADS, DK_A, DV_A), F32)
    n0 = jnp.zeros((bp, N_HEADS, DK_A), F32)
    m0 = jnp.zeros((bp, 1, LANES), F32)
    x1, c_p, n_p, m_p = _mlstm(qv, kt, o, gi, gf, gt, xp, c0, n0, m0, wout_a, lng[0, 0], lnb[0, 0],
                             batch=bp, seq=sp, tc=512, ch=ch)
    x2 = _mlp(x1, wup[0], wdn[0], lng[0, 1], lnb[0, 1], 1024, 1024)
    q, k, v, kt, vb, means = _qkv_b(x2, w_qkv_b, 512)
    x3 = _moba_prompt(q, kt, vb, means.reshape(bp, sp // BLOCK, D_MODEL), x2, wout_b, lng[1, 0], lnb[1, 0],
                      batch=bp, seq=sp)
    y_prompt = _mlp(x3, wup[1], wdn[1], lng[1, 1], lnb[1, 1], 1024, 1024)

    rows_s = bs * ss
    xs = x_sample.reshape(rows_s, D_MODEL)
    qv_s, kt_s, o_s, gi_s, gf_s, _ = _proj_a(xs, wm_a, wg_a, bg_a, rows_s, ch)
    pad3 = lambda a, val=0.0: jnp.pad(a.reshape(bs, ss, -1), ((0, 0), (0, ch - ss), (0, 0)),
                                      constant_values=val)
    flat = lambda a: a.reshape(bs * ch, -1)
    gi_pad = jnp.where(jnp.arange(LANES)[None, None, :] < N_HEADS, pad3(gi_s, NEG_INF), 0.0)
    gf_pad = pad3(gf_s)
    gt_pad = jnp.concatenate([jnp.swapaxes(gi_pad, 1, 2)[:, :N_HEADS],
                              jnp.swapaxes(gf_pad, 1, 2)[:, :N_HEADS]], axis=1)
    m0s = jnp.pad(state_m[0], ((0, 0), (0, LANES - N_HEADS))).reshape(bs, 1, LANES)
    kt_pad = jnp.pad(kt_s.reshape(QK_W, bs, ss), ((0, 0), (0, 0), (0, ch - ss))).reshape(QK_W, bs * ch)
    x1s, c_s, n_s, m_s = _mlstm(qv_s.astype(F32), kt_pad, o_s, flat(gi_pad), flat(gf_pad), gt_pad,
                                xs, state_C[0], state_n[0], m0s, wout_a, lng[0, 0], lnb[0, 0],
                                batch=bs, seq=ch, tc=ch, ch=ch, valid=ss)
    x2s = _mlp(x1s, wup[0], wdn[0], lng[0, 1], lnb[0, 1], rows_s, 512)
    q_s, k_s, v_s, _, _, _ = _qkv_b(x2s, w_qkv_b, rows_s)
    n_pool = cache_k.shape[0]
    att_s = _moba_sample(page_table, q_s.astype(F32), k_s, v_s,
                         cache_k.reshape(n_pool, PAGE * N_HEADS, HD_B),
                         cache_v.reshape(n_pool, PAGE * N_HEADS, HD_B), batch=bs, dec_seq=ss)
    x3s = _out_ln(att_s, wout_b, x2s, lng[1, 0], lnb[1, 0], rows_s)
    y_sample = _mlp(x3s, wup[1], wdn[1], lng[1, 1], lnb[1, 1], rows_s, 512)

    p_c, p_n, p_m = c_p[None], n_p[None], m_p[None, :, 0, :N_HEADS]
    s_c, s_n, s_m = c_s[None], n_s[None], m_s[None, :, 0, :N_HEADS]
    return (y_prompt.reshape(bp, sp, D_MODEL), y_sample.reshape(bs, ss, D_MODEL),
            p_c, p_n, p_m,
            k.reshape(bp, sp, N_HEADS, HD_B), v.reshape(bp, sp, N_HEADS, HD_B),
            s_c, s_n, s_m,
            k_s.reshape(bs, ss, N_HEADS, HD_B), v_s.reshape(bs, ss, N_HEADS, HD_B))
```

```python
import functools

import numpy as np
import jax
import jax.numpy as jnp
from jax import lax
from jax.experimental import pallas as pl
from jax.experimental.pallas import tpu as pltpu

F32 = jnp.float32
BF16 = jnp.bfloat16

D_MODEL = 1024
N_HEADS = 8
DK_A = 64
DV_A = 128
QK_W = N_HEADS * DK_A
V_W = N_HEADS * DV_A
HD_B = 128
BLOCK = 256
TOPK = 3
PAGE = 128
D_FF = 4 * D_MODEL
DEPTH = 2
ALPHA = (2.0 * DEPTH) ** 0.25
LN_EPS = 1e-5
NEG_INF = float("-inf")

LANES = 128
VMEM_LIMIT = 56 * 1024 * 1024

MLSTM_CHUNK = 128
M_FLOOR = -1e30
LOG2E = 1.4426950408889634


def _cparams(n_axes, flags=None):
    return pltpu.CompilerParams(dimension_semantics=("arbitrary",) * n_axes,
                                vmem_limit_bytes=VMEM_LIMIT, flags=flags)


def _dot(a, b):
    return jnp.dot(a, b, preferred_element_type=F32)


def _dot_nt(a, b):
    return lax.dot_general(a, b, (((1,), (1,)), ((), ())), preferred_element_type=F32)


def _dot_tn(a, b):
    return lax.dot_general(a, b, (((0,), (0,)), ((), ())), preferred_element_type=F32)


def _split3(x):
    x1 = x.astype(BF16)
    r1 = x - x1.astype(F32)
    x2 = r1.astype(BF16)
    x3 = (r1 - x2.astype(F32)).astype(BF16)
    return x1, x2, x3


def _layer_norm(y, g, b):
    mu = jnp.mean(y, axis=-1, keepdims=True)
    d = y - mu
    var = jnp.mean(d * d, axis=-1, keepdims=True)
    return d * lax.rsqrt(var + LN_EPS) * g + b


def _cummax_rows(x):
    row = lax.broadcasted_iota(jnp.int32, x.shape, 0)
    shift = 1
    while shift < x.shape[0]:
        x = jnp.maximum(x, jnp.where(row >= shift, pltpu.roll(x, shift, 0), NEG_INF))
        shift *= 2
    return x


def _proj_a_kernel(x_ref, wm_ref, wg_ref, bg_ref, qv_ref, kt_ref, o_ref, gi_ref, gf_ref, gt_ref, *, ch):
    x = x_ref[...]
    xh = x.astype(BF16)
    nq = 2 * QK_W + V_W
    step = 512
    for c0 in range(0, nq, step):
        acc = _dot(xh, wm_ref[:, c0:c0 + step])
        if c0 == QK_W:
            kt_ref[...] = (acc * (DK_A ** -0.5)).T.astype(BF16)
        else:
            dst = c0 if c0 < QK_W else c0 - QK_W
            qv_ref[:, dst:dst + step] = acc.astype(BF16)
    for c0 in range(0, V_W, step):
        o_ref[:, c0:c0 + step] = _dot(xh, wm_ref[:, nq + c0:nq + c0 + step])
    xl = (x - xh.astype(F32)).astype(BF16)
    wg = wg_ref[...]
    wgh = wg.astype(BF16)
    wgl = (wg - wgh.astype(F32)).astype(BF16)
    g = _dot(xh, wgh) + _dot(xl, wgh) + _dot(xh, wgl) + bg_ref[...]
    lane = lax.broadcasted_iota(jnp.int32, g.shape, 1)
    log_f = jnp.minimum(g, 0.0) - jnp.log1p(jnp.exp(-jnp.abs(g)))
    gi_ref[...] = jnp.where(lane < N_HEADS, g, 0.0)
    gf_ref[...] = jnp.where(lane < N_HEADS, pltpu.roll(log_f, LANES - N_HEADS, 1), 0.0)
    both_t = jnp.where(lane < N_HEADS, g, jnp.where(lane < 2 * N_HEADS, log_f, 0.0)).T
    for c in range(x.shape[0] // ch):
        gt_ref[c] = both_t[0:2 * N_HEADS, c * ch:(c + 1) * ch]


def _proj_a(x, wm, wg, bg, tm, ch):
    rows = x.shape[0]
    return pl.pallas_call(
        functools.partial(_proj_a_kernel, ch=ch),
        grid=(rows // tm,),
        in_specs=[pl.BlockSpec((tm, D_MODEL), lambda i: (i, 0)),
                  pl.BlockSpec(wm.shape, lambda i: (0, 0)),
                  pl.BlockSpec(wg.shape, lambda i: (0, 0)),
                  pl.BlockSpec(bg.shape, lambda i: (0, 0))],
        out_specs=[pl.BlockSpec((tm, QK_W + V_W), lambda i: (i, 0)),
                   pl.BlockSpec((QK_W, tm), lambda i: (0, i)),
                   pl.BlockSpec((tm, V_W), lambda i: (i, 0)),
                   pl.BlockSpec((tm, LANES), lambda i: (i, 0)),
                   pl.BlockSpec((tm, LANES), lambda i: (i, 0)),
                   pl.BlockSpec((tm // ch, 2 * N_HEADS, ch), lambda i: (i, 0, 0))],
        out_shape=[jax.ShapeDtypeStruct((rows, QK_W + V_W), BF16),
                   jax.ShapeDtypeStruct((QK_W, rows), BF16),
                   jax.ShapeDtypeStruct((rows, V_W), F32),
                   jax.ShapeDtypeStruct((rows, LANES), F32),
                   jax.ShapeDtypeStruct((rows, LANES), F32),
                   jax.ShapeDtypeStruct((rows // ch, 2 * N_HEADS, ch), F32)],
        compiler_params=_cparams(1),
        name="mlstm_in_proj",
    )(x, wm, wg, bg)


def _mlstm_kernel(qv_ref, kt_ref, o_ref, gi_ref, gf_ref, gt_ref, x_ref, c0_ref, n0_ref, m0_ref, wout_ref,
                  lng_ref, lnb_ref, y_ref, ct_ref, nt_ref, mt_ref, caug_s, m_s, h_s, *, tc, ch, valid):
    t = pl.program_id(1)

    def slab(ref, r0, c0, c1, dtype):
        if valid == ch:
            return ref[r0:r0 + ch, c0:c1]
        return jnp.concatenate([ref[:, c0:c1], jnp.zeros((ch - valid, c1 - c0), F32)], axis=0).astype(dtype)

    eye_k = (lax.broadcasted_iota(jnp.int32, (DK_A, DK_A), 0)
             == lax.broadcasted_iota(jnp.int32, (DK_A, DK_A), 1))

    @pl.when(t == 0)
    def _():
        for h in range(N_HEADS):
            caug_s[h, :, :DV_A] = c0_ref[0, h]
            n_row = jnp.broadcast_to(n0_ref[0, h:h + 1, :], (DK_A, DK_A))
            n_col = jnp.sum(jnp.where(eye_k, n_row, 0.0), axis=1, keepdims=True)
            caug_s[h, :, DV_A:] = jnp.broadcast_to(n_col, (DK_A, DV_A))
        m_s[...] = m0_ref[0]

    row = lax.broadcasted_iota(jnp.int32, (ch, ch), 0)
    col = lax.broadcasted_iota(jnp.int32, (ch, ch), 1)
    tril = row >= col
    tri_b = jnp.where(tril, 1.0, 0.0).astype(BF16)
    tri_t = jnp.where(row <= col, 1.0, 0.0).astype(BF16)
    ones_blk = jnp.ones((ch, LANES), BF16)
    trow = lax.broadcasted_iota(jnp.int32, (2 * N_HEADS, ch), 0)

    for j in range(tc // ch):
        r0 = j * ch
        gi = gi_ref[r0:r0 + ch, :]
        f1, f2, f3 = _split3(gf_ref[r0:r0 + ch, :])
        b = _dot(tri_b, f1) + _dot(tri_b, f2) + _dot(tri_b, f3)
        u = gi - b
        m_prev = m_s[...]
        big_m = jnp.maximum(_cummax_rows(u), m_prev)
        a_all = jnp.exp(m_prev - big_m)
        e_all = jnp.exp(-(b + big_m))
        m_end = big_m[ch - 1:ch, :]
        ae_all = a_all[ch - 1:ch, :]
        gt = gt_ref[j]
        r1, r2, r3 = _split3(jnp.where(trow >= N_HEADS, gt, 0.0))
        b_t = _dot(r1, tri_t) + _dot(r2, tri_t) + _dot(r3, tri_t)
        u_t = gt[0:N_HEADS, :] - b_t[N_HEADS:2 * N_HEADS, :]
        for h in range(N_HEADS):
            w = jnp.exp(jnp.where(tril, u_t[h:h + 1, :] - big_m[:, h:h + 1], NEG_INF))
            q = slab(qv_ref, r0, h * DK_A, (h + 1) * DK_A, BF16)
            k_t = kt_ref[h * DK_A:(h + 1) * DK_A, r0:r0 + ch]
            v = slab(qv_ref, r0, QK_W + h * DV_A, QK_W + (h + 1) * DV_A, BF16)
            vaug = jnp.concatenate([v, ones_blk], axis=1)
            sw = _dot(q, k_t) * w
            qc = _dot(q, caug_s[h].astype(BF16))
            a = jnp.broadcast_to(a_all[:, h:h + 1], (ch, LANES))
            tot = _dot(sw.astype(BF16), vaug) + jnp.concatenate([a, a], axis=1) * qc
            floor = jnp.broadcast_to(e_all[:, h:h + 1], (ch, LANES))
            rden = 1.0 / jnp.maximum(jnp.abs(tot[:, DV_A:]), floor)
            og = slab(o_ref, r0, h * DV_A, (h + 1) * DV_A, F32)
            hh = tot[:, :DV_A] * rden * (1.0 / (1.0 + jnp.exp(-og)))
            h_s[r0:r0 + ch, h * DV_A:(h + 1) * DV_A] = hh.astype(BF16)
            ws_t = jnp.exp(u_t[h:h + 1, :] - m_end[:, h:h + 1])
            kw_t = (k_t.astype(F32) * ws_t).astype(BF16)
            caug_s[h] = ae_all[:, h:h + 1] * caug_s[h] + _dot(kw_t, vaug)
        m_s[...] = b[ch - 1:ch, :] + m_end

    if valid == ch:
        y = _dot(h_s[...], wout_ref[...])
    else:
        y = _dot(h_s[0:16, :], wout_ref[...])[0:valid]
    y_ref[...] = _layer_norm(ALPHA * x_ref[...] + y, lng_ref[...], lnb_ref[...])

    @pl.when(t == pl.num_programs(1) - 1)
    def _():
        for h in range(N_HEADS):
            ct_ref[0, h] = caug_s[h, :, :DV_A]
            n_col = jnp.broadcast_to(caug_s[h, :, DV_A:DV_A + 1], (DK_A, DK_A))
            nt_ref[0, h:h + 1, :] = jnp.sum(jnp.where(eye_k, n_col, 0.0), axis=0, keepdims=True)
        mt_ref[0] = m_s[...]


def _mlstm(qv, kt, o, gi, gf, gt, x, c0, n0, m0, wout, lng, lnb, *, batch, seq, tc, ch, valid=None):
    n_t = seq // tc
    valid = ch if valid is None else valid
    tr = tc if valid == ch else valid
    kern = functools.partial(_mlstm_kernel, tc=tc, ch=ch, valid=valid)
    rmap = lambda b, t: (b * n_t + t, 0)
    cmap = lambda b, t: (0, 0)
    smap = lambda b, t: (b, 0, 0, 0)
    mmap = lambda b, t: (b, 0, 0)
    return pl.pallas_call(
        kern,
        grid=(batch, n_t),
        in_specs=[pl.BlockSpec((tr, QK_W + V_W), rmap),
                  pl.BlockSpec((QK_W, tc), lambda b, t: (0, b * n_t + t)),
                  pl.BlockSpec((tr, V_W), rmap),
                  pl.BlockSpec((tc, LANES), rmap),
                  pl.BlockSpec((tc, LANES), rmap),
                  pl.BlockSpec((tc // ch, 2 * N_HEADS, ch), lambda b, t: (b * n_t + t, 0, 0)),
                  pl.BlockSpec((tr, D_MODEL), rmap),
                  pl.BlockSpec((1, N_HEADS, DK_A, DV_A), smap),
                  pl.BlockSpec((1, N_HEADS, DK_A), mmap),
                  pl.BlockSpec((1, 1, LANES), mmap),
                  pl.BlockSpec(wout.shape, cmap),
                  pl.BlockSpec(lng.shape, cmap),
                  pl.BlockSpec(lnb.shape, cmap)],
        out_specs=[pl.BlockSpec((tr, D_MODEL), rmap),
                   pl.BlockSpec((1, N_HEADS, DK_A, DV_A), smap),
                   pl.BlockSpec((1, N_HEADS, DK_A), mmap),
                   pl.BlockSpec((1, 1, LANES), mmap)],
        out_shape=[jax.ShapeDtypeStruct((batch * n_t * tr, D_MODEL), F32),
                   jax.ShapeDtypeStruct((batch, N_HEADS, DK_A, DV_A), F32),
                   jax.ShapeDtypeStruct((batch, N_HEADS, DK_A), F32),
                   jax.ShapeDtypeStruct((batch, 1, LANES), F32)],
        scratch_shapes=[pltpu.VMEM((N_HEADS, DK_A, 2 * DV_A), F32),
                        pltpu.VMEM((1, LANES), F32),
                        pltpu.VMEM((tc, V_W), BF16)],
        compiler_params=_cparams(2),
        name="mlstm_scan",
    )(qv, kt, o, gi, gf, gt, x, c0, n0, m0, wout, lng, lnb)


def _mlp_kernel(x_ref, wup_ref, wdn_ref, lng_ref, lnb_ref, y_ref, acc_s, xb_s):
    j = pl.program_id(1)

    @pl.when(j == 0)
    def _():
        xb_s[...] = x_ref[...].astype(BF16)
        acc_s[...] = jnp.zeros_like(acc_s)

    hid = jnp.maximum(_dot(xb_s[...], wup_ref[...]), 0.0)
    acc_s[...] += _dot((hid * hid).astype(BF16), wdn_ref[...])

    @pl.when(j == pl.num_programs(1) - 1)
    def _():
        y_ref[...] = _layer_norm(ALPHA * x_ref[...] + acc_s[...], lng_ref[...], lnb_ref[...])


def _mlp(x, wup, wdn, lng, lnb, tm, tf, name="mlp_ln"):
    rows = x.shape[0]
    return pl.pallas_call(
        _mlp_kernel,
        grid=(rows // tm, D_FF // tf),
        in_specs=[pl.BlockSpec((tm, D_MODEL), lambda i, j: (i, 0)),
                  pl.BlockSpec((D_MODEL, tf), lambda i, j: (0, j)),
                  pl.BlockSpec((tf, D_MODEL), lambda i, j: (j, 0)),
                  pl.BlockSpec(lng.shape, lambda i, j: (0, 0)),
                  pl.BlockSpec(lnb.shape, lambda i, j: (0, 0))],
        out_specs=pl.BlockSpec((tm, D_MODEL), lambda i, j: (i, 0)),
        out_shape=jax.ShapeDtypeStruct((rows, D_MODEL), F32),
        scratch_shapes=[pltpu.VMEM((tm, D_MODEL), F32), pltpu.VMEM((tm, D_MODEL), BF16)],
        compiler_params=_cparams(2),
        name=name,
    )(x, wup, wdn, lng, lnb)


def _qkv_b_kernel(x_ref, w_ref, q_ref, k_ref, v_ref, kt_ref, vb_ref, mean_ref, *, tm):
    xh = x_ref[...].astype(BF16)
    step = 512
    for c0 in range(0, D_MODEL, step):
        q_ref[:, c0:c0 + step] = _dot(xh, w_ref[:, c0:c0 + step]).astype(BF16)
        kk = _dot(xh, w_ref[:, D_MODEL + c0:D_MODEL + c0 + step])
        k_ref[:, c0:c0 + step] = kk
        for hh in range(step // HD_B):
            for r in range(tm // BLOCK):
                kt_ref[c0 // HD_B + hh, r] = kk[r * BLOCK:(r + 1) * BLOCK, hh * HD_B:(hh + 1) * HD_B].T.astype(BF16)
        for r in range(tm // BLOCK):
            mean_ref[r, :, c0:c0 + step] = jnp.mean(kk[r * BLOCK:(r + 1) * BLOCK], axis=0, keepdims=True)
        vv = _dot(xh, w_ref[:, 2 * D_MODEL + c0:2 * D_MODEL + c0 + step])
        v_ref[:, c0:c0 + step] = vv
        vb_ref[:, c0:c0 + step] = vv.astype(BF16)


def _qkv_b(x, w, tm):
    rows = x.shape[0]
    rmap = lambda i: (i, 0)
    rspec = pl.BlockSpec((tm, D_MODEL), rmap)
    return pl.pallas_call(
        functools.partial(_qkv_b_kernel, tm=tm),
        grid=(rows // tm,),
        in_specs=[rspec, pl.BlockSpec(w.shape, lambda i: (0, 0))],
        out_specs=[rspec, rspec, rspec,
                   pl.BlockSpec((N_HEADS, tm // BLOCK, HD_B, BLOCK), lambda i: (0, i, 0, 0)),
                   rspec,
                   pl.BlockSpec((tm // BLOCK, 1, D_MODEL), lambda i: (i, 0, 0))],
        out_shape=[jax.ShapeDtypeStruct((rows, D_MODEL), BF16),
                   jax.ShapeDtypeStruct((rows, D_MODEL), F32),
                   jax.ShapeDtypeStruct((rows, D_MODEL), F32),
                   jax.ShapeDtypeStruct((N_HEADS, rows // BLOCK, HD_B, BLOCK), BF16),
                   jax.ShapeDtypeStruct((rows, D_MODEL), BF16),
                   jax.ShapeDtypeStruct((rows // BLOCK, 1, D_MODEL), F32)],
        compiler_params=_cparams(1),
        name="moba_qkv_proj",
    )(x, w)


def _out_ln_kernel(a_ref, w_ref, x_ref, lng_ref, lnb_ref, y_ref):
    y = _dot(a_ref[...].astype(BF16), w_ref[...])
    y_ref[...] = _layer_norm(ALPHA * x_ref[...] + y, lng_ref[...], lnb_ref[...])


def _out_ln(a, w, x, lng, lnb, tm):
    rows = x.shape[0]
    rspec = pl.BlockSpec((tm, D_MODEL), lambda i: (i, 0))
    cmap = lambda i: (0, 0)
    return pl.pallas_call(
        _out_ln_kernel,
        grid=(rows // tm,),
        in_specs=[rspec, pl.BlockSpec(w.shape, cmap), rspec,
                  pl.BlockSpec(lng.shape, cmap), pl.BlockSpec(lnb.shape, cmap)],
        out_specs=rspec,
        out_shape=jax.ShapeDtypeStruct((rows, D_MODEL), F32),
        compiler_params=_cparams(1),
        name="attn_out_ln",
    )(a, w, x, lng, lnb)


def _head_slopes(rep):
    slopes = np.zeros((1, LANES), np.float32)
    slopes[0, :N_HEADS * rep] = np.repeat(2.0 ** -np.arange(1, N_HEADS + 1, dtype=np.float64), rep)
    return jnp.asarray(slopes)


def _head_block_diag(rows_8):
    tiled = jnp.concatenate([rows_8] * (LANES // 8), axis=0)
    r = lax.broadcasted_iota(jnp.int32, (LANES, D_MODEL), 0)
    c = lax.broadcasted_iota(jnp.int32, (LANES, D_MODEL), 1)
    keep = (r // 8) == (c // HD_B)
    return jnp.where(keep, tiled, 0.0).astype(BF16)


def _moba_prompt_kernel(q_ref, kt_ref, vb_ref, mean_ref, slope_ref, x_ref, wout_ref, lng_ref, lnb_ref, o_ref,
                        sel_s, mask_s, m_s, l_s, acc_s, s_s):
    qi = pl.program_id(1)
    scale = HD_B ** -0.5
    rows = BLOCK

    @pl.when(jnp.logical_and(pl.program_id(0) == 0, qi == 0))
    def _():
        t_i = lax.broadcasted_iota(jnp.int32, (BLOCK, BLOCK), 0)
        j_i = lax.broadcasted_iota(jnp.int32, (BLOCK, BLOCK), 1)
        jf = j_i.astype(F32)
        for h in range(N_HEADS):
            mask_s[0, h] = (LOG2E * 2.0 ** -(h + 1)) * jf
            mask_s[1, h] = jnp.where(j_i > t_i, NEG_INF, (LOG2E * 2.0 ** -(h + 1)) * jf)

    mbd = _head_block_diag(mean_ref[0])
    sc = _dot_nt(q_ref[...], mbd)
    lane = lax.broadcasted_iota(jnp.int32, sc.shape, 1)
    blk = lane % 8
    past = jnp.logical_and(lane < N_HEADS * 8, blk < qi)
    sm = jnp.where(past, sc, NEG_INF)
    rank = jnp.zeros(sc.shape, F32)
    for r in range(1, 8):
        lo = pltpu.roll(sm, r, 1)
        rank += jnp.where(blk >= r, jnp.where(lo >= sm, 1.0, 0.0), 0.0)
        hi = pltpu.roll(sm, LANES - r, 1)
        rank += jnp.where(blk + r < 8, jnp.where(hi > sm, 1.0, 0.0), 0.0)
    sel_s[...] = jnp.where(past, jnp.where(rank < TOPK, 1.0, 0.0), 0.0).astype(BF16)
    m_s[...] = jnp.full(m_s.shape, M_FLOOR, F32)
    l_s[...] = jnp.zeros_like(l_s)
    acc_s[...] = jnp.zeros_like(acc_s)

    er = lax.broadcasted_iota(jnp.int32, (LANES, LANES), 0)
    ec = lax.broadcasted_iota(jnp.int32, (LANES, LANES), 1)
    ones_v = jnp.ones((BLOCK, HD_B), BF16)

    def key_block(kn, carry):
        pick = jnp.where(jnp.logical_and(er == ec * 8 + kn, ec < N_HEADS), 1.0, 0.0).astype(BF16)
        selcols = _dot(sel_s[...], pick)
        kbase = slope_ref[...] * (LOG2E * (kn * BLOCK).astype(F32))
        own = kn == qi
        biasc = jnp.where(jnp.logical_or(selcols > 0.5, own), kbase, NEG_INF)
        own_i = own.astype(jnp.int32)
        r0 = pl.multiple_of(kn * BLOCK, BLOCK)

        for h in range(N_HEADS):
            cs = slice(h * HD_B, (h + 1) * HD_B)
            s_s[h] = _dot(q_ref[:, cs], kt_ref[h, kn])

        for h in range(N_HEADS):
            cs = slice(h * HD_B, (h + 1) * HD_B)
            vaug = jnp.concatenate([vb_ref[pl.ds(r0, BLOCK), cs], ones_v], axis=1)
            bias = jnp.broadcast_to(biasc[:, h:h + 1], (rows, LANES))
            lg = s_s[h] * (scale * LOG2E) + mask_s[own_i, h] + jnp.concatenate([bias, bias], axis=1)
            m_old = m_s[h]
            m_new = jnp.maximum(m_old, jnp.max(lg, axis=1, keepdims=True))
            alpha = jnp.exp2(m_old - m_new)
            pr = jnp.exp2(lg - jnp.concatenate([m_new, m_new], axis=1))
            pv = _dot(pr.astype(BF16), vaug)
            l_s[h] = alpha * l_s[h] + pv[:, HD_B:]
            acc_s[:, cs] = alpha * acc_s[:, cs] + pv[:, :HD_B]
            m_s[h] = m_new
        return carry

    lax.fori_loop(0, qi + 1, key_block, 0)

    for h in range(N_HEADS):
        cs = slice(h * HD_B, (h + 1) * HD_B)
        acc_s[:, cs] = acc_s[:, cs] / l_s[h]
    y = _dot(acc_s[...].astype(BF16), wout_ref[...])
    o_ref[...] = _layer_norm(ALPHA * x_ref[...] + y, lng_ref[...], lnb_ref[...])


def _moba_prompt(q, kt, vb, means, x, wout, lng, lnb, *, batch, seq):
    nb = seq // BLOCK
    qmap = lambda b, i: (b * nb + i, 0)
    cmap = lambda b, i: (0, 0)
    return pl.pallas_call(
        _moba_prompt_kernel,
        grid=(batch, nb),
        in_specs=[pl.BlockSpec((BLOCK, D_MODEL), qmap),
                  pl.BlockSpec((N_HEADS, nb, HD_B, BLOCK), lambda b, i: (0, b, 0, 0)),
                  pl.BlockSpec((seq, D_MODEL), lambda b, i: (b, 0)),
                  pl.BlockSpec((1, nb, D_MODEL), lambda b, i: (b, 0, 0)),
                  pl.BlockSpec((1, LANES), cmap),
                  pl.BlockSpec((BLOCK, D_MODEL), qmap),
                  pl.BlockSpec(wout.shape, cmap),
                  pl.BlockSpec(lng.shape, cmap),
                  pl.BlockSpec(lnb.shape, cmap)],
        out_specs=pl.BlockSpec((BLOCK, D_MODEL), qmap),
        out_shape=jax.ShapeDtypeStruct((batch * seq, D_MODEL), F32),
        scratch_shapes=[pltpu.VMEM((BLOCK, LANES), BF16),
                        pltpu.VMEM((2, N_HEADS, BLOCK, BLOCK), F32),
                        pltpu.VMEM((N_HEADS, BLOCK, LANES), F32),
                        pltpu.VMEM((N_HEADS, BLOCK, LANES), F32),
                        pltpu.VMEM((BLOCK, D_MODEL), F32),
                        pltpu.VMEM((N_HEADS, BLOCK, BLOCK), F32)],
        compiler_params=_cparams(2),
        name="moba_prompt_attn",
    )(q, kt, vb, means, _head_slopes(1), x, wout, lng, lnb)


PAGE_SLOTS = 16
PAGE_GROUP = 8


def _moba_sample_kernel(pt_ref, q_ref, kn_ref, vn_ref, ck_ref, cv_ref, o_ref,
                        buf_s, sem, lt_s, po_s, rden_s, acc_s, *, n_pages, dec_seq, batch):
    b = pl.program_id(0)
    visits = 2 * n_pages
    total = batch * visits
    base = b * visits
    past_len = n_pages * PAGE
    n_blocks = past_len // BLOCK
    ppb = BLOCK // PAGE
    scale = HD_B ** -0.5

    def page_copy(src_page_ref, slot):
        return pltpu.make_async_copy(src_page_ref, buf_s.at[slot], sem.at[slot])

    def start_fetch(g):
        gb = g // visits
        gv = g % visits
        slot = g % PAGE_SLOTS

        @pl.when(gv < n_pages)
        def _():
            page_copy(ck_ref.at[pt_ref[gb, gv]], slot).start()

        @pl.when(gv >= n_pages)
        def _():
            page_copy(cv_ref.at[pt_ref[gb, gv - n_pages]], slot).start()

    def wait_fetch(slot):
        page_copy(ck_ref.at[0], slot).wait()

    def head_rows(slot, h):
        return buf_s[slot, pl.ds(h, PAGE, stride=N_HEADS), :]

    def next_fetch(v):
        nxt = base + v + PAGE_SLOTS

        @pl.when(nxt < total)
        def _():
            start_fetch(nxt)

    def pad_page(new_ref, h):
        blk = new_ref[:, h * HD_B:(h + 1) * HD_B]
        return jnp.concatenate([blk, jnp.zeros((PAGE - dec_seq, HD_B), F32)], axis=0).astype(BF16)

    def q_head(h):
        return q_ref[:, h * HD_B:(h + 1) * HD_B].astype(BF16)

    @pl.when(b == 0)
    def _():
        for g in range(PAGE_SLOTS):
            start_fetch(jnp.int32(g))

    def k_group(it, carry):
        pages = [it * PAGE_GROUP + i for i in range(PAGE_GROUP)]
        for pg in pages:
            wait_fetch(pg % PAGE_SLOTS)
        for pg in pages:
            for h in range(N_HEADS):
                k_t = head_rows(pg % PAGE_SLOTS, h).T.astype(BF16)
                lt_s[h, pg] = _dot(q_head(h), k_t)
        for pg in pages:
            next_fetch(pg)
        return carry

    lax.fori_loop(0, n_pages // PAGE_GROUP, k_group, 0)

    def softmax_weights():
        lane = lax.broadcasted_iota(jnp.int32, (dec_seq, LANES), 1)
        rows = []
        for h in range(N_HEADS):
            blk_sum = jnp.sum(lt_s[h].reshape(n_blocks, ppb, dec_seq, PAGE), axis=1)
            tot = jnp.sum(blk_sum, axis=2, keepdims=True) * (1.0 / BLOCK)
            sc_h = jnp.full((dec_seq, LANES), NEG_INF, F32)
            for n in range(n_blocks):
                sc_h = jnp.where(lane == n, tot[n], sc_h)
            rows.append(sc_h)
        sc = jnp.concatenate(rows, axis=0)
        lane_a = lax.broadcasted_iota(jnp.int32, sc.shape, 1)
        rank = jnp.zeros(sc.shape, F32)
        for r in range(1, n_blocks):
            lo = pltpu.roll(sc, r, 1)
            rank += jnp.where(lane_a >= r, jnp.where(lo >= sc, 1.0, 0.0), 0.0)
            hi = pltpu.roll(sc, LANES - r, 1)
            rank += jnp.where(lane_a + r < n_blocks, jnp.where(hi > sc, 1.0, 0.0), 0.0)
        sel_bias = jnp.where(jnp.logical_and(rank < TOPK, lane_a < n_blocks), 0.0, NEG_INF)
        kpos = (lax.broadcasted_iota(jnp.int32, (n_pages, 1, PAGE), 0) * PAGE
                + lax.broadcasted_iota(jnp.int32, (n_pages, 1, PAGE), 2) - past_len).astype(F32)
        qi = lax.broadcasted_iota(jnp.int32, (dec_seq, PAGE), 0)
        ki = lax.broadcasted_iota(jnp.int32, (dec_seq, PAGE), 1)
        for h in range(N_HEADS):
            slope = 2.0 ** -(h + 1)
            bias_h = sel_bias[h * dec_seq:(h + 1) * dec_seq, :]
            bias_pages = jnp.stack([jnp.broadcast_to(bias_h[:, n:n + 1], (dec_seq, PAGE))
                                    for n in range(n_blocks) for _ in range(ppb)], axis=0)
            lgs = lt_s[h] * scale + slope * kpos + bias_pages
            lgo = _dot_nt(q_head(h), pad_page(kn_ref, h))
            lgo = jnp.where(ki <= qi, lgo * scale + slope * ki.astype(F32), NEG_INF)
            mx = jnp.maximum(jnp.max(jnp.max(lgs, axis=0), axis=1, keepdims=True),
                             jnp.max(lgo, axis=1, keepdims=True))
            pr = jnp.exp(lgs - mx[None])
            po = jnp.exp(lgo - mx)
            den = jnp.sum(jnp.sum(pr, axis=0), axis=1, keepdims=True) + jnp.sum(po, axis=1, keepdims=True)
            lt_s[h] = pr
            po_s[h] = po
            rden_s[h] = jnp.broadcast_to(1.0 / den, (dec_seq, LANES))

    softmax_weights()
    for h in range(N_HEADS):
        acc_s[:, h * HD_B:(h + 1) * HD_B] = _dot(po_s[h].astype(BF16), pad_page(vn_ref, h))

    def v_group(it, carry):
        pages = [it * PAGE_GROUP + i for i in range(PAGE_GROUP)]
        for pg in pages:
            wait_fetch(pg % PAGE_SLOTS)
        for pg in pages:
            for h in range(N_HEADS):
                acc_s[:, h * HD_B:(h + 1) * HD_B] += _dot(lt_s[h, pg].astype(BF16),
                                                          head_rows(pg % PAGE_SLOTS, h).astype(BF16))
        for pg in pages:
            next_fetch(n_pages + pg)
        return carry

    lax.fori_loop(0, n_pages // PAGE_GROUP, v_group, 0)

    for h in range(N_HEADS):
        cs = slice(h * HD_B, (h + 1) * HD_B)
        o_ref[:, cs] = acc_s[:, cs] * rden_s[h]


def _moba_sample(page_table, q, k_new, v_new, cache_k, cache_v, *, batch, dec_seq):
    n_pages = page_table.shape[1]
    assert n_pages % PAGE_GROUP == 0 and (2 * n_pages) % PAGE_SLOTS == 0 and PAGE_SLOTS % PAGE_GROUP == 0
    rspec = pl.BlockSpec((dec_seq, D_MODEL), lambda b, pt: (b, 0))
    grid_spec = pltpu.PrefetchScalarGridSpec(
        num_scalar_prefetch=1,
        grid=(batch,),
        in_specs=[rspec, rspec, rspec,
                  pl.BlockSpec(memory_space=pl.ANY), pl.BlockSpec(memory_space=pl.ANY)],
        out_specs=rspec,
        scratch_shapes=[pltpu.VMEM((PAGE_SLOTS, PAGE * N_HEADS, HD_B), F32),
                        pltpu.SemaphoreType.DMA((PAGE_SLOTS,)),
                        pltpu.VMEM((N_HEADS, n_pages, dec_seq, PAGE), F32),
                        pltpu.VMEM((N_HEADS, dec_seq, PAGE), F32),
                        pltpu.VMEM((N_HEADS, dec_seq, LANES), F32),
                        pltpu.VMEM((dec_seq, D_MODEL), F32)])
    kern = functools.partial(_moba_sample_kernel, n_pages=n_pages, dec_seq=dec_seq, batch=batch)
    return pl.pallas_call(
        kern,
        grid_spec=grid_spec,
        out_shape=jax.ShapeDtypeStruct((batch * dec_seq, D_MODEL), F32),
        compiler_params=_cparams(1),
        name="moba_sample_attn",
    )(page_table, q, k_new, v_new, cache_k, cache_v)


def kernel(x_prompt, x_sample, state_C, state_n, state_m, cache_k, cache_v, page_table,
           w_in_a, b_gate_a, w_out_a, w_q_b, w_out_b, w_kv, w_up, w_down, ln_g, ln_b):
    bp, sp, _ = x_prompt.shape
    bs, ss, _ = x_sample.shape
    ch = MLSTM_CHUNK
    nq = 2 * QK_W + 2 * V_W

    wm_a = w_in_a[0, :, :nq].astype(BF16)
    wg_a = jnp.pad(w_in_a[0, :, nq:], ((0, 0), (0, LANES - 2 * N_HEADS)))
    bg_a = jnp.pad(b_gate_a[0], (0, LANES - 2 * N_HEADS)).reshape(1, LANES)
    wout_a = w_out_a[0].astype(BF16)
    w_qkv_b = jnp.concatenate([w_q_b[0], w_kv], axis=1).astype(BF16)
    wout_b = w_out_b[0].astype(BF16)
    wup = w_up.astype(BF16)
    wdn = w_down.astype(BF16)
    lng = ln_g.reshape(DEPTH, 2, 1, D_MODEL)
    lnb = ln_b.reshape(DEPTH, 2, 1, D_MODEL)

    xp = x_prompt.reshape(bp * sp, D_MODEL)
    qv, kt, o, gi, gf, gt = _proj_a(xp, wm_a, wg_a, bg_a, 1024, ch)
    c0 = jnp.zeros((bp, N_HEADS, DK_A, DV_A), F32)
    n0 = jnp.zeros((bp, N_HEADS, DK_A), F32)
    m0 = jnp.zeros((bp, 1, LANES), F32)
    x1, c_p, n_p, m_p = _mlstm(qv, kt, o, gi, gf, gt, xp, c0, n0, m0, wout_a, lng[0, 0], lnb[0, 0],
                             batch=bp, seq=sp, tc=512, ch=ch)
    x2 = _mlp(x1, wup[0], wdn[0], lng[0, 1], lnb[0, 1], 1024, 1024)
    q, k, v, kt, vb, means = _qkv_b(x2, w_qkv_b, 512)
    x3 = _moba_prompt(q, kt, vb, means.reshape(bp, sp // BLOCK, D_MODEL), x2, wout_b, lng[1, 0], lnb[1, 0],
                      batch=bp, seq=sp)
    y_prompt = _mlp(x3, wup[1], wdn[1], lng[1, 1], lnb[1, 1], 1024, 1024)

    rows_s = bs * ss
    xs = x_sample.reshape(rows_s, D_MODEL)
    qv_s, kt_s, o_s, gi_s, gf_s, _ = _proj_a(xs, wm_a, wg_a, bg_a, rows_s, ch)
    pad3 = lambda a, val=0.0: jnp.pad(a.reshape(bs, ss, -1), ((0, 0), (0, ch - ss), (0, 0)),
                                      constant_values=val)
    flat = lambda a: a.reshape(bs * ch, -1)
    gi_pad = jnp.where(jnp.arange(LANES)[None, None, :] < N_HEADS, pad3(gi_s, NEG_INF), 0.0)
    gf_pad = pad3(gf_s)
    gt_pad = jnp.concatenate([jnp.swapaxes(gi_pad, 1, 2)[:, :N_HEADS],
                              jnp.swapaxes(gf_pad, 1, 2)[:, :N_HEADS]], axis=1)
    m0s = jnp.pad(state_m[0], ((0, 0), (0, LANES - N_HEADS))).reshape(bs, 1, LANES)
    kt_pad = jnp.pad(kt_s.reshape(QK_W, bs, ss), ((0, 0), (0, 0), (0, ch - ss))).reshape(QK_W, bs * ch)
    x1s, c_s, n_s, m_s = _mlstm(qv_s.astype(F32), kt_pad, o_s, flat(gi_pad), flat(gf_pad), gt_pad,
                                xs, state_C[0], state_n[0], m0s, wout_a, lng[0, 0], lnb[0, 0],
                                batch=bs, seq=ch, tc=ch, ch=ch, valid=ss)
    x2s = _mlp(x1s, wup[0], wdn[0], lng[0, 1], lnb[0, 1], rows_s, 512)
    q_s, k_s, v_s, _, _, _ = _qkv_b(x2s, w_qkv_b, rows_s)
    n_pool = cache_k.shape[0]
    att_s = _moba_sample(page_table, q_s.astype(F32), k_s, v_s,
                         cache_k.reshape(n_pool, PAGE * N_HEADS, HD_B),
                         cache_v.reshape(n_pool, PAGE * N_HEADS, HD_B), batch=bs, dec_seq=ss)
    x3s = _out_ln(att_s, wout_b, x2s, lng[1, 0], lnb[1, 0], rows_s)
    y_sample = _mlp(x3s, wup[1], wdn[1], lng[1, 1], lnb[1, 1], rows_s, 512)

    p_c, p_n, p_m = c_p[None], n_p[None], m_p[None, :, 0, :N_HEADS]
    s_c, s_n, s_m = c_s[None], n_s[None], m_s[None, :, 0, :N_HEADS]
    return (y_prompt.reshape(bp, sp, D_MODEL), y_sample.reshape(bs, ss, D_MODEL),
            p_c, p_n, p_m,
            k.reshape(bp, sp, N_HEADS, HD_B), v.reshape(bp, sp, N_HEADS, HD_B),
            s_c, s_n, s_m,
            k_s.reshape(bs, ss, N_HEADS, HD_B), v_s.reshape(bs, ss, N_HEADS, HD_B))
```

```python
import functools

import numpy as np
import jax
import jax.numpy as jnp
from jax import lax
from jax.experimental import pallas as pl
from jax.experimental.pallas import tpu as pltpu

F32 = jnp.float32
BF16 = jnp.bfloat16

D_MODEL = 1024
N_HEADS = 8
DK_A = 64
DV_A = 128
QK_W = N_HEADS * DK_A
V_W = N_HEADS * DV_A
HD_B = 128
BLOCK = 256
TOPK = 3
PAGE = 128
D_FF = 4 * D_MODEL
DEPTH = 2
ALPHA = (2.0 * DEPTH) ** 0.25
LN_EPS = 1e-5
NEG_INF = float("-inf")

LANES = 128
VMEM_LIMIT = 56 * 1024 * 1024

MLSTM_CHUNK = 128
M_FLOOR = -1e30
LOG2E = 1.4426950408889634


def _cparams(n_axes, flags=None):
    return pltpu.CompilerParams(dimension_semantics=("arbitrary",) * n_axes,
                                vmem_limit_bytes=VMEM_LIMIT, flags=flags)


def _dot(a, b):
    return jnp.dot(a, b, preferred_element_type=F32)


def _dot_nt(a, b):
    return lax.dot_general(a, b, (((1,), (1,)), ((), ())), preferred_element_type=F32)


def _dot_tn(a, b):
    return lax.dot_general(a, b, (((0,), (0,)), ((), ())), preferred_element_type=F32)


def _split3(x):
    x1 = x.astype(BF16)
    r1 = x - x1.astype(F32)
    x2 = r1.astype(BF16)
    x3 = (r1 - x2.astype(F32)).astype(BF16)
    return x1, x2, x3


def _layer_norm(y, g, b):
    mu = jnp.mean(y, axis=-1, keepdims=True)
    d = y - mu
    var = jnp.mean(d * d, axis=-1, keepdims=True)
    return d * lax.rsqrt(var + LN_EPS) * g + b


def _cummax_rows(x):
    row = lax.broadcasted_iota(jnp.int32, x.shape, 0)
    shift = 1
    while shift < x.shape[0]:
        x = jnp.maximum(x, jnp.where(row >= shift, pltpu.roll(x, shift, 0), NEG_INF))
        shift *= 2
    return x


def _proj_a_kernel(x_ref, wm_ref, wg_ref, bg_ref, qv_ref, kt_ref, o_ref, gi_ref, gf_ref, gt_ref, *, ch):
    x = x_ref[...]
    xh = x.astype(BF16)
    nq = 2 * QK_W + V_W
    step = 512
    for c0 in range(0, nq, step):
        acc = _dot(xh, wm_ref[:, c0:c0 + step])
        if c0 == QK_W:
            kt_ref[...] = (acc * (DK_A ** -0.5)).T.astype(BF16)
        else:
            dst = c0 if c0 < QK_W else c0 - QK_W
            qv_ref[:, dst:dst + step] = acc.astype(BF16)
    for c0 in range(0, V_W, step):
        o_ref[:, c0:c0 + step] = _dot(xh, wm_ref[:, nq + c0:nq + c0 + step])
    xl = (x - xh.astype(F32)).astype(BF16)
    wg = wg_ref[...]
    wgh = wg.astype(BF16)
    wgl = (wg - wgh.astype(F32)).astype(BF16)
    g = _dot(xh, wgh) + _dot(xl, wgh) + _dot(xh, wgl) + bg_ref[...]
    lane = lax.broadcasted_iota(jnp.int32, g.shape, 1)
    log_f = jnp.minimum(g, 0.0) - jnp.log1p(jnp.exp(-jnp.abs(g)))
    gi_ref[...] = jnp.where(lane < N_HEADS, g, 0.0)
    gf_ref[...] = jnp.where(lane < N_HEADS, pltpu.roll(log_f, LANES - N_HEADS, 1), 0.0)
    both_t = jnp.where(lane < N_HEADS, g, jnp.where(lane < 2 * N_HEADS, log_f, 0.0)).T
    for c in range(x.shape[0] // ch):
        gt_ref[c] = both_t[0:2 * N_HEADS, c * ch:(c + 1) * ch]


def _proj_a(x, wm, wg, bg, tm, ch):
    rows = x.shape[0]
    return pl.pallas_call(
        functools.partial(_proj_a_kernel, ch=ch),
        grid=(rows // tm,),
        in_specs=[pl.BlockSpec((tm, D_MODEL), lambda i: (i, 0)),
                  pl.BlockSpec(wm.shape, lambda i: (0, 0)),
                  pl.BlockSpec(wg.shape, lambda i: (0, 0)),
                  pl.BlockSpec(bg.shape, lambda i: (0, 0))],
        out_specs=[pl.BlockSpec((tm, QK_W + V_W), lambda i: (i, 0)),
                   pl.BlockSpec((QK_W, tm), lambda i: (0, i)),
                   pl.BlockSpec((tm, V_W), lambda i: (i, 0)),
                   pl.BlockSpec((tm, LANES), lambda i: (i, 0)),
                   pl.BlockSpec((tm, LANES), lambda i: (i, 0)),
                   pl.BlockSpec((tm // ch, 2 * N_HEADS, ch), lambda i: (i, 0, 0))],
        out_shape=[jax.ShapeDtypeStruct((rows, QK_W + V_W), BF16),
                   jax.ShapeDtypeStruct((QK_W, rows), BF16),
                   jax.ShapeDtypeStruct((rows, V_W), F32),
                   jax.ShapeDtypeStruct((rows, LANES), F32),
                   jax.ShapeDtypeStruct((rows, LANES), F32),
                   jax.ShapeDtypeStruct((rows // ch, 2 * N_HEADS, ch), F32)],
        compiler_params=_cparams(1),
        name="mlstm_in_proj",
    )(x, wm, wg, bg)


def _mlstm_kernel(qv_ref, kt_ref, o_ref, gi_ref, gf_ref, gt_ref, x_ref, c0_ref, n0_ref, m0_ref, wout_ref,
                  lng_ref, lnb_ref, y_ref, ct_ref, nt_ref, mt_ref, caug_s, m_s, h_s, *, tc, ch, valid):
    t = pl.program_id(1)

    def slab(ref, r0, c0, c1, dtype):
        if valid == ch:
            return ref[r0:r0 + ch, c0:c1]
        return jnp.concatenate([ref[:, c0:c1], jnp.zeros((ch - valid, c1 - c0), F32)], axis=0).astype(dtype)

    eye_k = (lax.broadcasted_iota(jnp.int32, (DK_A, DK_A), 0)
             == lax.broadcasted_iota(jnp.int32, (DK_A, DK_A), 1))

    @pl.when(t == 0)
    def _():
        for h in range(N_HEADS):
            caug_s[h, :, :DV_A] = c0_ref[0, h]
            n_row = jnp.broadcast_to(n0_ref[0, h:h + 1, :], (DK_A, DK_A))
            n_col = jnp.sum(jnp.where(eye_k, n_row, 0.0), axis=1, keepdims=True)
            caug_s[h, :, DV_A:] = jnp.broadcast_to(n_col, (DK_A, DV_A))
        m_s[...] = m0_ref[0]

    row = lax.broadcasted_iota(jnp.int32, (ch, ch), 0)
    col = lax.broadcasted_iota(jnp.int32, (ch, ch), 1)
    tril = row >= col
    tri_b = jnp.where(tril, 1.0, 0.0).astype(BF16)
    tri_t = jnp.where(row <= col, 1.0, 0.0).astype(BF16)
    ones_blk = jnp.ones((ch, LANES), BF16)
    trow = lax.broadcasted_iota(jnp.int32, (2 * N_HEADS, ch), 0)

    for j in range(tc // ch):
        r0 = j * ch
        gi = gi_ref[r0:r0 + ch, :]
        f1, f2, f3 = _split3(gf_ref[r0:r0 + ch, :])
        b = _dot(tri_b, f1) + _dot(tri_b, f2) + _dot(tri_b, f3)
        u = gi - b
        m_prev = m_s[...]
        big_m = jnp.maximum(_cummax_rows(u), m_prev)
        a_all = jnp.exp(m_prev - big_m)
        e_all = jnp.exp(-(b + big_m))
        m_end = big_m[ch - 1:ch, :]
        ae_all = a_all[ch - 1:ch, :]
        gt = gt_ref[j]
        r1, r2, r3 = _split3(jnp.where(trow >= N_HEADS, gt, 0.0))
        b_t = _dot(r1, tri_t) + _dot(r2, tri_t) + _dot(r3, tri_t)
        u_t = gt[0:N_HEADS, :] - b_t[N_HEADS:2 * N_HEADS, :]
        for h in range(N_HEADS):
            w = jnp.exp(jnp.where(tril, u_t[h:h + 1, :] - big_m[:, h:h + 1], NEG_INF))
            q = slab(qv_ref, r0, h * DK_A, (h + 1) * DK_A, BF16)
            k_t = kt_ref[h * DK_A:(h + 1) * DK_A, r0:r0 + ch]
            v = slab(qv_ref, r0, QK_W + h * DV_A, QK_W + (h + 1) * DV_A, BF16)
            vaug = jnp.concatenate([v, ones_blk], axis=1)
            sw = _dot(q, k_t) * w
            qc = _dot(q, caug_s[h].astype(BF16))
            a = jnp.broadcast_to(a_all[:, h:h + 1], (ch, LANES))
            tot = _dot(sw.astype(BF16), vaug) + jnp.concatenate([a, a], axis=1) * qc
            floor = jnp.broadcast_to(e_all[:, h:h + 1], (ch, LANES))
            rden = 1.0 / jnp.maximum(jnp.abs(tot[:, DV_A:]), floor)
            og = slab(o_ref, r0, h * DV_A, (h + 1) * DV_A, F32)
            hh = tot[:, :DV_A] * rden * (1.0 / (1.0 + jnp.exp(-og)))
            h_s[r0:r0 + ch, h * DV_A:(h + 1) * DV_A] = hh.astype(BF16)
            ws_t = jnp.exp(u_t[h:h + 1, :] - m_end[:, h:h + 1])
            kw_t = (k_t.astype(F32) * ws_t).astype(BF16)
            caug_s[h] = ae_all[:, h:h + 1] * caug_s[h] + _dot(kw_t, vaug)
        m_s[...] = b[ch - 1:ch, :] + m_end

    if valid == ch:
        y = _dot(h_s[...], wout_ref[...])
    else:
        y = _dot(h_s[0:16, :], wout_ref[...])[0:valid]
    y_ref[...] = _layer_norm(ALPHA * x_ref[...] + y, lng_ref[...], lnb_ref[...])

    @pl.when(t == pl.num_programs(1) - 1)
    def _():
        for h in range(N_HEADS):
            ct_ref[0, h] = caug_s[h, :, :DV_A]
            n_col = jnp.broadcast_to(caug_s[h, :, DV_A:DV_A + 1], (DK_A, DK_A))
            nt_ref[0, h:h + 1, :] = jnp.sum(jnp.where(eye_k, n_col, 0.0), axis=0, keepdims=True)
        mt_ref[0] = m_s[...]


def _mlstm(qv, kt, o, gi, gf, gt, x, c0, n0, m0, wout, lng, lnb, *, batch, seq, tc, ch, valid=None):
    n_t = seq // tc
    valid = ch if valid is None else valid
    tr = tc if valid == ch else valid
    kern = functools.partial(_mlstm_kernel, tc=tc, ch=ch, valid=valid)
    rmap = lambda b, t: (b * n_t + t, 0)
    cmap = lambda b, t: (0, 0)
    smap = lambda b, t: (b, 0, 0, 0)
    mmap = lambda b, t: (b, 0, 0)
    return pl.pallas_call(
        kern,
        grid=(batch, n_t),
        in_specs=[pl.BlockSpec((tr, QK_W + V_W), rmap),
                  pl.BlockSpec((QK_W, tc), lambda b, t: (0, b * n_t + t)),
                  pl.BlockSpec((tr, V_W), rmap),
                  pl.BlockSpec((tc, LANES), rmap),
                  pl.BlockSpec((tc, LANES), rmap),
                  pl.BlockSpec((tc // ch, 2 * N_HEADS, ch), lambda b, t: (b * n_t + t, 0, 0)),
                  pl.BlockSpec((tr, D_MODEL), rmap),
                  pl.BlockSpec((1, N_HEADS, DK_A, DV_A), smap),
                  pl.BlockSpec((1, N_HEADS, DK_A), mmap),
                  pl.BlockSpec((1, 1, LANES), mmap),
                  pl.BlockSpec(wout.shape, cmap),
                  pl.BlockSpec(lng.shape, cmap),
                  pl.BlockSpec(lnb.shape, cmap)],
        out_specs=[pl.BlockSpec((tr, D_MODEL), rmap),
                   pl.BlockSpec((1, N_HEADS, DK_A, DV_A), smap),
                   pl.BlockSpec((1, N_HEADS, DK_A), mmap),
                   pl.BlockSpec((1, 1, LANES), mmap)],
        out_shape=[jax.ShapeDtypeStruct((batch * n_t * tr, D_MODEL), F32),
                   jax.ShapeDtypeStruct((batch, N_HEADS, DK_A, DV_A), F32),
                   jax.ShapeDtypeStruct((batch, N_HEADS, DK_A), F32),
                   jax.ShapeDtypeStruct((batch, 1, LANES), F32)],
        scratch_shapes=[pltpu.VMEM((N_HEADS, DK_A, 2 * DV_A), F32),
                        pltpu.VMEM((1, LANES), F32),
                        pltpu.VMEM((tc, V_W), BF16)],
        compiler_params=_cparams(2),
        name="mlstm_scan",
    )(qv, kt, o, gi, gf, gt, x, c0, n0, m0, wout, lng, lnb)


def _mlp_kernel(x_ref, wup_ref, wdn_ref, lng_ref, lnb_ref, y_ref, acc_s, xb_s):
    j = pl.program_id(1)

    @pl.when(j == 0)
    def _():
        xb_s[...] = x_ref[...].astype(BF16)
        acc_s[...] = jnp.zeros_like(acc_s)

    hid = jnp.maximum(_dot(xb_s[...], wup_ref[...]), 0.0)
    acc_s[...] += _dot((hid * hid).astype(BF16), wdn_ref[...])

    @pl.when(j == pl.num_programs(1) - 1)
    def _():
        y_ref[...] = _layer_norm(ALPHA * x_ref[...] + acc_s[...], lng_ref[...], lnb_ref[...])


def _mlp(x, wup, wdn, lng, lnb, tm, tf, name="mlp_ln"):
    rows = x.shape[0]
    return pl.pallas_call(
        _mlp_kernel,
        grid=(rows // tm, D_FF // tf),
        in_specs=[pl.BlockSpec((tm, D_MODEL), lambda i, j: (i, 0)),
                  pl.BlockSpec((D_MODEL, tf), lambda i, j: (0, j)),
                  pl.BlockSpec((tf, D_MODEL), lambda i, j: (j, 0)),
                  pl.BlockSpec(lng.shape, lambda i, j: (0, 0)),
                  pl.BlockSpec(lnb.shape, lambda i, j: (0, 0))],
        out_specs=pl.BlockSpec((tm, D_MODEL), lambda i, j: (i, 0)),
        out_shape=jax.ShapeDtypeStruct((rows, D_MODEL), F32),
        scratch_shapes=[pltpu.VMEM((tm, D_MODEL), F32), pltpu.VMEM((tm, D_MODEL), BF16)],
        compiler_params=_cparams(2),
        name=name,
    )(x, wup, wdn, lng, lnb)


def _qkv_b_kernel(x_ref, wq_ref, wkv_ref, q_ref, k_ref, v_ref, kt_ref, vb_ref, mean_ref, *, tm):
    xh = x_ref[...].astype(BF16)
    step = 512
    for c0 in range(0, D_MODEL, step):
        q_ref[:, c0:c0 + step] = _dot(xh, wq_ref[:, c0:c0 + step]).astype(BF16)
        kk = _dot(xh, wkv_ref[:, c0:c0 + step])
        k_ref[:, c0:c0 + step] = kk
        for hh in range(step // HD_B):
            for r in range(tm // BLOCK):
                kt_ref[c0 // HD_B + hh, r] = kk[r * BLOCK:(r + 1) * BLOCK, hh * HD_B:(hh + 1) * HD_B].T.astype(BF16)
        for r in range(tm // BLOCK):
            mean_ref[r, :, c0:c0 + step] = jnp.mean(kk[r * BLOCK:(r + 1) * BLOCK], axis=0, keepdims=True)
        vv = _dot(xh, wkv_ref[:, D_MODEL + c0:D_MODEL + c0 + step])
        v_ref[:, c0:c0 + step] = vv
        vb_ref[:, c0:c0 + step] = vv.astype(BF16)


def _qkv_b(x, wq, wkv, tm):
    rows = x.shape[0]
    rmap = lambda i: (i, 0)
    rspec = pl.BlockSpec((tm, D_MODEL), rmap)
    return pl.pallas_call(
        functools.partial(_qkv_b_kernel, tm=tm),
        grid=(rows // tm,),
        in_specs=[rspec, pl.BlockSpec(wq.shape, lambda i: (0, 0)), pl.BlockSpec(wkv.shape, lambda i: (0, 0))],
        out_specs=[rspec, rspec, rspec,
                   pl.BlockSpec((N_HEADS, tm // BLOCK, HD_B, BLOCK), lambda i: (0, i, 0, 0)),
                   rspec,
                   pl.BlockSpec((tm // BLOCK, 1, D_MODEL), lambda i: (i, 0, 0))],
        out_shape=[jax.ShapeDtypeStruct((rows, D_MODEL), BF16),
                   jax.ShapeDtypeStruct((rows, D_MODEL), F32),
                   jax.ShapeDtypeStruct((rows, D_MODEL), F32),
                   jax.ShapeDtypeStruct((N_HEADS, rows // BLOCK, HD_B, BLOCK), BF16),
                   jax.ShapeDtypeStruct((rows, D_MODEL), BF16),
                   jax.ShapeDtypeStruct((rows // BLOCK, 1, D_MODEL), F32)],
        compiler_params=_cparams(1),
        name="moba_qkv_proj",
    )(x, wq, wkv)


def _out_ln_kernel(a_ref, w_ref, x_ref, lng_ref, lnb_ref, y_ref):
    y = _dot(a_ref[...].astype(BF16), w_ref[...])
    y_ref[...] = _layer_norm(ALPHA * x_ref[...] + y, lng_ref[...], lnb_ref[...])


def _out_ln(a, w, x, lng, lnb, tm):
    rows = x.shape[0]
    rspec = pl.BlockSpec((tm, D_MODEL), lambda i: (i, 0))
    cmap = lambda i: (0, 0)
    return pl.pallas_call(
        _out_ln_kernel,
        grid=(rows // tm,),
        in_specs=[rspec, pl.BlockSpec(w.shape, cmap), rspec,
                  pl.BlockSpec(lng.shape, cmap), pl.BlockSpec(lnb.shape, cmap)],
        out_specs=rspec,
        out_shape=jax.ShapeDtypeStruct((rows, D_MODEL), F32),
        compiler_params=_cparams(1),
        name="attn_out_ln",
    )(a, w, x, lng, lnb)


def _head_slopes(rep):
    slopes = np.zeros((1, LANES), np.float32)
    slopes[0, :N_HEADS * rep] = np.repeat(2.0 ** -np.arange(1, N_HEADS + 1, dtype=np.float64), rep)
    return jnp.asarray(slopes)


def _head_block_diag(rows_8):
    tiled = jnp.concatenate([rows_8] * (LANES // 8), axis=0)
    r = lax.broadcasted_iota(jnp.int32, (LANES, D_MODEL), 0)
    c = lax.broadcasted_iota(jnp.int32, (LANES, D_MODEL), 1)
    keep = (r // 8) == (c // HD_B)
    return jnp.where(keep, tiled, 0.0).astype(BF16)


def _moba_prompt_kernel(q_ref, kt_ref, vb_ref, mean_ref, slope_ref, x_ref, wout_ref, lng_ref, lnb_ref, o_ref,
                        sel_s, mask_s, m_s, l_s, acc_s, s_s):
    qi = pl.program_id(1)
    scale = HD_B ** -0.5
    rows = BLOCK

    @pl.when(jnp.logical_and(pl.program_id(0) == 0, qi == 0))
    def _():
        t_i = lax.broadcasted_iota(jnp.int32, (BLOCK, BLOCK), 0)
        j_i = lax.broadcasted_iota(jnp.int32, (BLOCK, BLOCK), 1)
        jf = j_i.astype(F32)
        for h in range(N_HEADS):
            mask_s[0, h] = (LOG2E * 2.0 ** -(h + 1)) * jf
            mask_s[1, h] = jnp.where(j_i > t_i, NEG_INF, (LOG2E * 2.0 ** -(h + 1)) * jf)

    mbd = _head_block_diag(mean_ref[0])
    sc = _dot_nt(q_ref[...], mbd)
    lane = lax.broadcasted_iota(jnp.int32, sc.shape, 1)
    blk = lane % 8
    past = jnp.logical_and(lane < N_HEADS * 8, blk < qi)
    sm = jnp.where(past, sc, NEG_INF)
    rank = jnp.zeros(sc.shape, F32)
    for r in range(1, 8):
        lo = pltpu.roll(sm, r, 1)
        rank += jnp.where(blk >= r, jnp.where(lo >= sm, 1.0, 0.0), 0.0)
        hi = pltpu.roll(sm, LANES - r, 1)
        rank += jnp.where(blk + r < 8, jnp.where(hi > sm, 1.0, 0.0), 0.0)
    sel_s[...] = jnp.where(past, jnp.where(rank < TOPK, 1.0, 0.0), 0.0).astype(BF16)
    m_s[...] = jnp.full(m_s.shape, M_FLOOR, F32)
    l_s[...] = jnp.zeros_like(l_s)
    acc_s[...] = jnp.zeros_like(acc_s)

    er = lax.broadcasted_iota(jnp.int32, (LANES, LANES), 0)
    ec = lax.broadcasted_iota(jnp.int32, (LANES, LANES), 1)
    ones_v = jnp.ones((BLOCK, HD_B), BF16)

    def key_block(kn, carry):
        pick = jnp.where(jnp.logical_and(er == ec * 8 + kn, ec < N_HEADS), 1.0, 0.0).astype(BF16)
        selcols = _dot(sel_s[...], pick)
        kbase = slope_ref[...] * (LOG2E * (kn * BLOCK).astype(F32))
        own = kn == qi
        biasc = jnp.where(jnp.logical_or(selcols > 0.5, own), kbase, NEG_INF)
        own_i = own.astype(jnp.int32)
        r0 = pl.multiple_of(kn * BLOCK, BLOCK)

        for h in range(N_HEADS):
            cs = slice(h * HD_B, (h + 1) * HD_B)
            s_s[h] = _dot(q_ref[:, cs], kt_ref[h, kn])

        for h in range(N_HEADS):
            cs = slice(h * HD_B, (h + 1) * HD_B)
            vaug = jnp.concatenate([vb_ref[pl.ds(r0, BLOCK), cs], ones_v], axis=1)
            bias = jnp.broadcast_to(biasc[:, h:h + 1], (rows, LANES))
            lg = s_s[h] * (scale * LOG2E) + mask_s[own_i, h] + jnp.concatenate([bias, bias], axis=1)
            m_old = m_s[h]
            m_new = jnp.maximum(m_old, jnp.max(lg, axis=1, keepdims=True))
            alpha = jnp.exp2(m_old - m_new)
            pr = jnp.exp2(lg - jnp.concatenate([m_new, m_new], axis=1))
            pv = _dot(pr.astype(BF16), vaug)
            l_s[h] = alpha * l_s[h] + pv[:, HD_B:]
            acc_s[:, cs] = alpha * acc_s[:, cs] + pv[:, :HD_B]
            m_s[h] = m_new
        return carry

    lax.fori_loop(0, qi + 1, key_block, 0)

    for h in range(N_HEADS):
        cs = slice(h * HD_B, (h + 1) * HD_B)
        acc_s[:, cs] = acc_s[:, cs] / l_s[h]
    y = _dot(acc_s[...].astype(BF16), wout_ref[...])
    o_ref[...] = _layer_norm(ALPHA * x_ref[...] + y, lng_ref[...], lnb_ref[...])


def _moba_prompt(q, kt, vb, means, x, wout, lng, lnb, *, batch, seq):
    nb = seq // BLOCK
    qmap = lambda b, i: (b * nb + i, 0)
    cmap = lambda b, i: (0, 0)
    return pl.pallas_call(
        _moba_prompt_kernel,
        grid=(batch, nb),
        in_specs=[pl.BlockSpec((BLOCK, D_MODEL), qmap),
                  pl.BlockSpec((N_HEADS, nb, HD_B, BLOCK), lambda b, i: (0, b, 0, 0)),
                  pl.BlockSpec((seq, D_MODEL), lambda b, i: (b, 0)),
                  pl.BlockSpec((1, nb, D_MODEL), lambda b, i: (b, 0, 0)),
                  pl.BlockSpec((1, LANES), cmap),
                  pl.BlockSpec((BLOCK, D_MODEL), qmap),
                  pl.BlockSpec(wout.shape, cmap),
                  pl.BlockSpec(lng.shape, cmap),
                  pl.BlockSpec(lnb.shape, cmap)],
        out_specs=pl.BlockSpec((BLOCK, D_MODEL), qmap),
        out_shape=jax.ShapeDtypeStruct((batch * seq, D_MODEL), F32),
        scratch_shapes=[pltpu.VMEM((BLOCK, LANES), BF16),
                        pltpu.VMEM((2, N_HEADS, BLOCK, BLOCK), F32),
                        pltpu.VMEM((N_HEADS, BLOCK, LANES), F32),
                        pltpu.VMEM((N_HEADS, BLOCK, LANES), F32),
                        pltpu.VMEM((BLOCK, D_MODEL), F32),
                        pltpu.VMEM((N_HEADS, BLOCK, BLOCK), F32)],
        compiler_params=_cparams(2),
        name="moba_prompt_attn",
    )(q, kt, vb, means, _head_slopes(1), x, wout, lng, lnb)


PAGES_PER_STEP = 16


def _moba_sample_kernel(ktab_ref, vtab_ref, q_ref, kn_ref, vn_ref, *refs, n_pages, dec_seq):
    kp_refs = refs[:PAGES_PER_STEP]
    vp_refs = refs[PAGES_PER_STEP:2 * PAGES_PER_STEP]
    o_ref = refs[2 * PAGES_PER_STEP]
    lt_s, po_s, rden_s, acc_s = refs[2 * PAGES_PER_STEP + 1:]
    ph = pl.program_id(1)
    s = pl.program_id(2)
    n_steps = pl.num_programs(2)
    past_len = n_pages * PAGE
    n_blocks = past_len // BLOCK
    ppb = BLOCK // PAGE
    scale = HD_B ** -0.5

    def head_rows(page_ref, h):
        return page_ref[0, pl.ds(h, PAGE, stride=N_HEADS), :]

    def pad_page(new_ref, h):
        blk = new_ref[:, h * HD_B:(h + 1) * HD_B]
        return jnp.concatenate([blk, jnp.zeros((PAGE - dec_seq, HD_B), F32)], axis=0).astype(BF16)

    def q_head(h):
        return q_ref[:, h * HD_B:(h + 1) * HD_B].astype(BF16)

    @pl.when(ph == 0)
    def _():
        for i in range(PAGES_PER_STEP):
            pg = s * PAGES_PER_STEP + i
            for h in range(N_HEADS):
                k_t = head_rows(kp_refs[i], h).T.astype(BF16)
                lt_s[h, pg] = _dot(q_head(h), k_t)

    @pl.when(jnp.logical_and(ph == 0, s == n_steps - 1))
    def _():
        lane = lax.broadcasted_iota(jnp.int32, (dec_seq, LANES), 1)
        rows = []
        for h in range(N_HEADS):
            blk_sum = jnp.sum(lt_s[h].reshape(n_blocks, ppb, dec_seq, PAGE), axis=1)
            tot = jnp.sum(blk_sum, axis=2, keepdims=True) * (1.0 / BLOCK)
            sc_h = jnp.full((dec_seq, LANES), NEG_INF, F32)
            for n in range(n_blocks):
                sc_h = jnp.where(lane == n, tot[n], sc_h)
            rows.append(sc_h)
        sc = jnp.concatenate(rows, axis=0)
        lane_a = lax.broadcasted_iota(jnp.int32, sc.shape, 1)
        rank = jnp.zeros(sc.shape, F32)
        for r in range(1, n_blocks):
            lo = pltpu.roll(sc, r, 1)
            rank += jnp.where(lane_a >= r, jnp.where(lo >= sc, 1.0, 0.0), 0.0)
            hi = pltpu.roll(sc, LANES - r, 1)
            rank += jnp.where(lane_a + r < n_blocks, jnp.where(hi > sc, 1.0, 0.0), 0.0)
        sel_bias = jnp.where(jnp.logical_and(rank < TOPK, lane_a < n_blocks), 0.0, NEG_INF)
        kpos = (lax.broadcasted_iota(jnp.int32, (n_pages, 1, PAGE), 0) * PAGE
                + lax.broadcasted_iota(jnp.int32, (n_pages, 1, PAGE), 2) - past_len).astype(F32)
        qi = lax.broadcasted_iota(jnp.int32, (dec_seq, PAGE), 0)
        ki = lax.broadcasted_iota(jnp.int32, (dec_seq, PAGE), 1)
        for h in range(N_HEADS):
            slope = 2.0 ** -(h + 1)
            bias_h = sel_bias[h * dec_seq:(h + 1) * dec_seq, :]
            bias_pages = jnp.stack([jnp.broadcast_to(bias_h[:, n:n + 1], (dec_seq, PAGE))
                                    for n in range(n_blocks) for _ in range(ppb)], axis=0)
            lgs = lt_s[h] * scale + slope * kpos + bias_pages
            lgo = _dot_nt(q_head(h), pad_page(kn_ref, h))
            lgo = jnp.where(ki <= qi, lgo * scale + slope * ki.astype(F32), NEG_INF)
            mx = jnp.maximum(jnp.max(jnp.max(lgs, axis=0), axis=1, keepdims=True),
                             jnp.max(lgo, axis=1, keepdims=True))
            pr = jnp.exp(lgs - mx[None])
            po = jnp.exp(lgo - mx)
            den = jnp.sum(jnp.sum(pr, axis=0), axis=1, keepdims=True) + jnp.sum(po, axis=1, keepdims=True)
            lt_s[h] = pr
            po_s[h] = po
            rden_s[h] = jnp.broadcast_to(1.0 / den, (dec_seq, LANES))

    @pl.when(jnp.logical_and(ph == 1, s == 0))
    def _():
        for h in range(N_HEADS):
            acc_s[:, h * HD_B:(h + 1) * HD_B] = _dot(po_s[h].astype(BF16), pad_page(vn_ref, h))

    @pl.when(ph == 1)
    def _():
        for i in range(PAGES_PER_STEP):
            pg = s * PAGES_PER_STEP + i
            for h in range(N_HEADS):
                acc_s[:, h * HD_B:(h + 1) * HD_B] += _dot(lt_s[h, pg].astype(BF16),
                                                          head_rows(vp_refs[i], h).astype(BF16))

    @pl.when(jnp.logical_and(ph == 1, s == n_steps - 1))
    def _():
        for h in range(N_HEADS):
            cs = slice(h * HD_B, (h + 1) * HD_B)
            o_ref[:, cs] = acc_s[:, cs] * rden_s[h]


def _moba_sample(page_table, q, k_new, v_new, cache_k, cache_v, *, batch, dec_seq):
    n_pages = page_table.shape[1]
    n_steps = n_pages // PAGES_PER_STEP
    rspec = pl.BlockSpec((dec_seq, D_MODEL), lambda b, ph, s, kt, vt: (b, 0))
    pt3 = page_table.reshape(batch, n_steps, PAGES_PER_STEP)
    last = jnp.broadcast_to(pt3[:, n_steps - 1:n_steps], pt3.shape)
    prev_last = jnp.concatenate([last[:1], last[:-1]], axis=0)
    k_tab = jnp.stack([pt3, last], axis=1).reshape(batch * 2 * n_steps, PAGES_PER_STEP)
    v_tab = jnp.stack([prev_last, pt3], axis=1).reshape(batch * 2 * n_steps, PAGES_PER_STEP)

    def kmap(i):
        return lambda b, ph, s, kt, vt: (kt[(b * 2 + ph) * n_steps + s, i], 0, 0)

    def vmap_(i):
        return lambda b, ph, s, kt, vt: (vt[(b * 2 + ph) * n_steps + s, i], 0, 0)

    pspec = lambda m: pl.BlockSpec((1, PAGE * N_HEADS, HD_B), m)
    grid_spec = pltpu.PrefetchScalarGridSpec(
        num_scalar_prefetch=2,
        grid=(batch, 2, n_steps),
        in_specs=[rspec, rspec, rspec]
                 + [pspec(kmap(i)) for i in range(PAGES_PER_STEP)]
                 + [pspec(vmap_(i)) for i in range(PAGES_PER_STEP)],
        out_specs=rspec,
        scratch_shapes=[pltpu.VMEM((N_HEADS, n_pages, dec_seq, PAGE), F32),
                        pltpu.VMEM((N_HEADS, dec_seq, PAGE), F32),
                        pltpu.VMEM((N_HEADS, dec_seq, LANES), F32),
                        pltpu.VMEM((dec_seq, D_MODEL), F32)])
    kern = functools.partial(_moba_sample_kernel, n_pages=n_pages, dec_seq=dec_seq)
    return pl.pallas_call(
        kern,
        grid_spec=grid_spec,
        out_shape=jax.ShapeDtypeStruct((batch * dec_seq, D_MODEL), F32),
        compiler_params=_cparams(3),
        name="moba_sample_attn",
    )(k_tab, v_tab, q, k_new, v_new, *([cache_k] * PAGES_PER_STEP), *([cache_v] * PAGES_PER_STEP))


def kernel(x_prompt, x_sample, state_C, state_n, state_m, cache_k, cache_v, page_table,
           w_in_a, b_gate_a, w_out_a, w_q_b, w_out_b, w_kv, w_up, w_down, ln_g, ln_b):
    bp, sp, _ = x_prompt.shape
    bs, ss, _ = x_sample.shape
    ch = MLSTM_CHUNK
    nq = 2 * QK_W + 2 * V_W

    wm_a = w_in_a[0].astype(BF16)
    wg_a = jnp.pad(w_in_a[0, :, nq:], ((0, 0), (0, LANES - 2 * N_HEADS)))
    bg_a = jnp.pad(b_gate_a[0], (0, LANES - 2 * N_HEADS)).reshape(1, LANES)
    wout_a = w_out_a[0].astype(BF16)
    wq_b = w_q_b[0].astype(BF16)
    wkv_b = w_kv.astype(BF16)
    wout_b = w_out_b[0].astype(BF16)
    wup = w_up.astype(BF16)
    wdn = w_down.astype(BF16)
    lng = ln_g.reshape(DEPTH, 2, 1, D_MODEL)
    lnb = ln_b.reshape(DEPTH, 2, 1, D_MODEL)

    xp = x_prompt.reshape(bp * sp, D_MODEL)
    qv, kt, o, gi, gf, gt = _proj_a(xp, wm_a, wg_a, bg_a, 1024, ch)
    c0 = jnp.zeros((bp, N_HEADS, DK_A, DV_A), F32)
    n0 = jnp.zeros((bp, N_HEADS, DK_A), F32)
    m0 = jnp.zeros((bp, 1, LANES), F32)
    x1, c_p, n_p, m_p = _mlstm(qv, kt, o, gi, gf, gt, xp, c0, n0, m0, wout_a, lng[0, 0], lnb[0, 0],
                             batch=bp, seq=sp, tc=512, ch=ch)
    x2 = _mlp(x1, wup[0], wdn[0], lng[0, 1], lnb[0, 1], 1024, 1024)
    q, k, v, kt, vb, means = _qkv_b(x2, wq_b, wkv_b, 512)
    x3 = _moba_prompt(q, kt, vb, means.reshape(bp, sp // BLOCK, D_MODEL), x2, wout_b, lng[1, 0], lnb[1, 0],
                      batch=bp, seq=sp)
    y_prompt = _mlp(x3, wup[1], wdn[1], lng[1, 1], lnb[1, 1], 1024, 1024)

    rows_s = bs * ss
    xs = x_sample.reshape(rows_s, D_MODEL)
    qv_s, kt_s, o_s, gi_s, gf_s, _ = _proj_a(xs, wm_a, wg_a, bg_a, rows_s, ch)
    pad3 = lambda a, val=0.0: jnp.pad(a.reshape(bs, ss, -1), ((0, 0), (0, ch - ss), (0, 0)),
                                      constant_values=val)
    flat = lambda a: a.reshape(bs * ch, -1)
    gi_pad = jnp.where(jnp.arange(LANES)[None, None, :] < N_HEADS, pad3(gi_s, NEG_INF), 0.0)
    gf_pad = pad3(gf_s)
    gt_pad = jnp.concatenate([jnp.swapaxes(gi_pad, 1, 2)[:, :N_HEADS],
                              jnp.swapaxes(gf_pad, 1, 2)[:, :N_HEADS]], axis=1)
    m0s = jnp.pad(state_m[0], ((0, 0), (0, LANES - N_HEADS))).reshape(bs, 1, LANES)
    kt_pad = jnp.pad(kt_s.reshape(QK_W, bs, ss), ((0, 0), (0, 0), (0, ch - ss))).reshape(QK_W, bs * ch)
    x1s, c_s, n_s, m_s = _mlstm(qv_s.astype(F32), kt_pad, o_s, flat(gi_pad), flat(gf_pad), gt_pad,
                                xs, state_C[0], state_n[0], m0s, wout_a, lng[0, 0], lnb[0, 0],
                                batch=bs, seq=ch, tc=ch, ch=ch, valid=ss)
    x2s = _mlp(x1s, wup[0], wdn[0], lng[0, 1], lnb[0, 1], rows_s, 512)
    q_s, k_s, v_s, _, _, _ = _qkv_b(x2s, wq_b, wkv_b, rows_s)
    n_pool = cache_k.shape[0]
    att_s = _moba_sample(page_table, q_s.astype(F32), k_s, v_s,
                         cache_k.reshape(n_pool, PAGE * N_HEADS, HD_B),
                         cache_v.reshape(n_pool, PAGE * N_HEADS, HD_B), batch=bs, dec_seq=ss)
    x3s = _out_ln(att_s, wout_b, x2s, lng[1, 0], lnb[1, 0], rows_s)
    y_sample = _mlp(x3s, wup[1], wdn[1], lng[1, 1], lnb[1, 1], rows_s, 512)

    p_c, p_n, p_m = c_p[None], n_p[None], m_p[None, :, 0, :N_HEADS]
    s_c, s_n, s_m = c_s[None], n_s[None], m_s[None, :, 0, :N_HEADS]
    return (y_prompt.reshape(bp, sp, D_MODEL), y_sample.reshape(bs, ss, D_MODEL),
            p_c, p_n, p_m,
            k.reshape(bp, sp, N_HEADS, HD_B), v.reshape(bp, sp, N_HEADS, HD_B),
            s_c, s_n, s_m,
            k_s.reshape(bs, ss, N_HEADS, HD_B), v_s.reshape(bs, ss, N_HEADS, HD_B))
```

```python
import functools

import numpy as np
import jax
import jax.numpy as jnp
from jax import lax
from jax.experimental import pallas as pl
from jax.experimental.pallas import tpu as pltpu

F32 = jnp.float32
BF16 = jnp.bfloat16

D_MODEL = 1024
N_HEADS = 8
DK_A = 64
DV_A = 128
QK_W = N_HEADS * DK_A
V_W = N_HEADS * DV_A
HD_B = 128
BLOCK = 256
TOPK = 3
PAGE = 128
D_FF = 4 * D_MODEL
DEPTH = 2
ALPHA = (2.0 * DEPTH) ** 0.25
LN_EPS = 1e-5
NEG_INF = float("-inf")

LANES = 128
VMEM_LIMIT = 56 * 1024 * 1024

MLSTM_CHUNK = 128
M_FLOOR = -1e30
LOG2E = 1.4426950408889634


def _cparams(n_axes, flags=None):
    return pltpu.CompilerParams(dimension_semantics=("arbitrary",) * n_axes,
                                vmem_limit_bytes=VMEM_LIMIT, flags=flags)


def _dot(a, b):
    return jnp.dot(a, b, preferred_element_type=F32)


def _dot_nt(a, b):
    return lax.dot_general(a, b, (((1,), (1,)), ((), ())), preferred_element_type=F32)


def _dot_tn(a, b):
    return lax.dot_general(a, b, (((0,), (0,)), ((), ())), preferred_element_type=F32)


def _split3(x):
    x1 = x.astype(BF16)
    r1 = x - x1.astype(F32)
    x2 = r1.astype(BF16)
    x3 = (r1 - x2.astype(F32)).astype(BF16)
    return x1, x2, x3


def _layer_norm(y, g, b):
    mu = jnp.mean(y, axis=-1, keepdims=True)
    d = y - mu
    var = jnp.mean(d * d, axis=-1, keepdims=True)
    return d * lax.rsqrt(var + LN_EPS) * g + b


def _cummax_rows(x):
    row = lax.broadcasted_iota(jnp.int32, x.shape, 0)
    shift = 1
    while shift < x.shape[0]:
        x = jnp.maximum(x, jnp.where(row >= shift, pltpu.roll(x, shift, 0), NEG_INF))
        shift *= 2
    return x


def _proj_a_kernel(x_ref, wm_ref, wg_ref, bg_ref, qv_ref, kt_ref, o_ref, gi_ref, gf_ref, gt_ref, *, ch):
    x = x_ref[...]
    xh = x.astype(BF16)
    nq = 2 * QK_W + V_W
    step = 512
    for c0 in range(0, nq, step):
        acc = _dot(xh, wm_ref[:, c0:c0 + step])
        if c0 == QK_W:
            kt_ref[...] = (acc * (DK_A ** -0.5)).T.astype(BF16)
        else:
            dst = c0 if c0 < QK_W else c0 - QK_W
            qv_ref[:, dst:dst + step] = acc.astype(BF16)
    for c0 in range(0, V_W, step):
        o_ref[:, c0:c0 + step] = _dot(xh, wm_ref[:, nq + c0:nq + c0 + step])
    xl = (x - xh.astype(F32)).astype(BF16)
    wg = wg_ref[...]
    wgh = wg.astype(BF16)
    wgl = (wg - wgh.astype(F32)).astype(BF16)
    g = _dot(xh, wgh) + _dot(xl, wgh) + _dot(xh, wgl) + bg_ref[...]
    lane = lax.broadcasted_iota(jnp.int32, g.shape, 1)
    log_f = jnp.minimum(g, 0.0) - jnp.log1p(jnp.exp(-jnp.abs(g)))
    gi_ref[...] = jnp.where(lane < N_HEADS, g, 0.0)
    gf_ref[...] = jnp.where(lane < N_HEADS, pltpu.roll(log_f, LANES - N_HEADS, 1), 0.0)
    both_t = jnp.where(lane < N_HEADS, g, jnp.where(lane < 2 * N_HEADS, log_f, 0.0)).T
    for c in range(x.shape[0] // ch):
        gt_ref[c] = both_t[0:2 * N_HEADS, c * ch:(c + 1) * ch]


def _proj_a(x, wm, wg, bg, tm, ch):
    rows = x.shape[0]
    return pl.pallas_call(
        functools.partial(_proj_a_kernel, ch=ch),
        grid=(rows // tm,),
        in_specs=[pl.BlockSpec((tm, D_MODEL), lambda i: (i, 0)),
                  pl.BlockSpec(wm.shape, lambda i: (0, 0)),
                  pl.BlockSpec(wg.shape, lambda i: (0, 0)),
                  pl.BlockSpec(bg.shape, lambda i: (0, 0))],
        out_specs=[pl.BlockSpec((tm, QK_W + V_W), lambda i: (i, 0)),
                   pl.BlockSpec((QK_W, tm), lambda i: (0, i)),
                   pl.BlockSpec((tm, V_W), lambda i: (i, 0)),
                   pl.BlockSpec((tm, LANES), lambda i: (i, 0)),
                   pl.BlockSpec((tm, LANES), lambda i: (i, 0)),
                   pl.BlockSpec((tm // ch, 2 * N_HEADS, ch), lambda i: (i, 0, 0))],
        out_shape=[jax.ShapeDtypeStruct((rows, QK_W + V_W), BF16),
                   jax.ShapeDtypeStruct((QK_W, rows), BF16),
                   jax.ShapeDtypeStruct((rows, V_W), F32),
                   jax.ShapeDtypeStruct((rows, LANES), F32),
                   jax.ShapeDtypeStruct((rows, LANES), F32),
                   jax.ShapeDtypeStruct((rows // ch, 2 * N_HEADS, ch), F32)],
        compiler_params=_cparams(1),
        name="mlstm_in_proj",
    )(x, wm, wg, bg)


def _mlstm_kernel(qv_ref, kt_ref, o_ref, gi_ref, gf_ref, gt_ref, x_ref, c0_ref, n0_ref, m0_ref, wout_ref,
                  lng_ref, lnb_ref, y_ref, ct_ref, nt_ref, mt_ref, caug_s, m_s, h_s, *, tc, ch, valid):
    t = pl.program_id(1)

    def slab(ref, r0, c0, c1, dtype):
        if valid == ch:
            return ref[r0:r0 + ch, c0:c1]
        return jnp.concatenate([ref[:, c0:c1], jnp.zeros((ch - valid, c1 - c0), F32)], axis=0).astype(dtype)

    eye_k = (lax.broadcasted_iota(jnp.int32, (DK_A, DK_A), 0)
             == lax.broadcasted_iota(jnp.int32, (DK_A, DK_A), 1))

    @pl.when(t == 0)
    def _():
        for h in range(N_HEADS):
            caug_s[h, :, :DV_A] = c0_ref[0, h]
            n_row = jnp.broadcast_to(n0_ref[0, h:h + 1, :], (DK_A, DK_A))
            n_col = jnp.sum(jnp.where(eye_k, n_row, 0.0), axis=1, keepdims=True)
            caug_s[h, :, DV_A:] = jnp.broadcast_to(n_col, (DK_A, DV_A))
        m_s[...] = m0_ref[0]

    row = lax.broadcasted_iota(jnp.int32, (ch, ch), 0)
    col = lax.broadcasted_iota(jnp.int32, (ch, ch), 1)
    tril = row >= col
    tri_b = jnp.where(tril, 1.0, 0.0).astype(BF16)
    tri_t = jnp.where(row <= col, 1.0, 0.0).astype(BF16)
    ones_blk = jnp.ones((ch, LANES), BF16)
    trow = lax.broadcasted_iota(jnp.int32, (2 * N_HEADS, ch), 0)

    for j in range(tc // ch):
        r0 = j * ch
        gi = gi_ref[r0:r0 + ch, :]
        f1, f2, f3 = _split3(gf_ref[r0:r0 + ch, :])
        b = _dot(tri_b, f1) + _dot(tri_b, f2) + _dot(tri_b, f3)
        u = gi - b
        m_prev = m_s[...]
        big_m = jnp.maximum(_cummax_rows(u), m_prev)
        a_all = jnp.exp(m_prev - big_m)
        e_all = jnp.exp(-(b + big_m))
        m_end = big_m[ch - 1:ch, :]
        ae_all = a_all[ch - 1:ch, :]
        gt = gt_ref[j]
        r1, r2, r3 = _split3(jnp.where(trow >= N_HEADS, gt, 0.0))
        b_t = _dot(r1, tri_t) + _dot(r2, tri_t) + _dot(r3, tri_t)
        u_t = gt[0:N_HEADS, :] - b_t[N_HEADS:2 * N_HEADS, :]
        for h in range(N_HEADS):
            w = jnp.exp(jnp.where(tril, u_t[h:h + 1, :] - big_m[:, h:h + 1], NEG_INF))
            q = slab(qv_ref, r0, h * DK_A, (h + 1) * DK_A, BF16)
            k_t = kt_ref[h * DK_A:(h + 1) * DK_A, r0:r0 + ch]
            v = slab(qv_ref, r0, QK_W + h * DV_A, QK_W + (h + 1) * DV_A, BF16)
            vaug = jnp.concatenate([v, ones_blk], axis=1)
            sw = _dot(q, k_t) * w
            qc = _dot(q, caug_s[h].astype(BF16))
            a = jnp.broadcast_to(a_all[:, h:h + 1], (ch, LANES))
            tot = _dot(sw.astype(BF16), vaug) + jnp.concatenate([a, a], axis=1) * qc
            floor = jnp.broadcast_to(e_all[:, h:h + 1], (ch, LANES))
            rden = 1.0 / jnp.maximum(jnp.abs(tot[:, DV_A:]), floor)
            og = slab(o_ref, r0, h * DV_A, (h + 1) * DV_A, F32)
            hh = tot[:, :DV_A] * rden * (1.0 / (1.0 + jnp.exp(-og)))
            h_s[r0:r0 + ch, h * DV_A:(h + 1) * DV_A] = hh.astype(BF16)
            ws_t = jnp.exp(u_t[h:h + 1, :] - m_end[:, h:h + 1])
            kw_t = (k_t.astype(F32) * ws_t).astype(BF16)
            caug_s[h] = ae_all[:, h:h + 1] * caug_s[h] + _dot(kw_t, vaug)
        m_s[...] = b[ch - 1:ch, :] + m_end

    if valid == ch:
        y = _dot(h_s[...], wout_ref[...])
    else:
        y = _dot(h_s[0:16, :], wout_ref[...])[0:valid]
    y_ref[...] = _layer_norm(ALPHA * x_ref[...] + y, lng_ref[...], lnb_ref[...])

    @pl.when(t == pl.num_programs(1) - 1)
    def _():
        for h in range(N_HEADS):
            ct_ref[0, h] = caug_s[h, :, :DV_A]
            n_col = jnp.broadcast_to(caug_s[h, :, DV_A:DV_A + 1], (DK_A, DK_A))
            nt_ref[0, h:h + 1, :] = jnp.sum(jnp.where(eye_k, n_col, 0.0), axis=0, keepdims=True)
        mt_ref[0] = m_s[...]


def _mlstm(qv, kt, o, gi, gf, gt, x, c0, n0, m0, wout, lng, lnb, *, batch, seq, tc, ch, valid=None):
    n_t = seq // tc
    valid = ch if valid is None else valid
    tr = tc if valid == ch else valid
    kern = functools.partial(_mlstm_kernel, tc=tc, ch=ch, valid=valid)
    rmap = lambda b, t: (b * n_t + t, 0)
    cmap = lambda b, t: (0, 0)
    smap = lambda b, t: (b, 0, 0, 0)
    mmap = lambda b, t: (b, 0, 0)
    return pl.pallas_call(
        kern,
        grid=(batch, n_t),
        in_specs=[pl.BlockSpec((tr, QK_W + V_W), rmap),
                  pl.BlockSpec((QK_W, tc), lambda b, t: (0, b * n_t + t)),
                  pl.BlockSpec((tr, V_W), rmap),
                  pl.BlockSpec((tc, LANES), rmap),
                  pl.BlockSpec((tc, LANES), rmap),
                  pl.BlockSpec((tc // ch, 2 * N_HEADS, ch), lambda b, t: (b * n_t + t, 0, 0)),
                  pl.BlockSpec((tr, D_MODEL), rmap),
                  pl.BlockSpec((1, N_HEADS, DK_A, DV_A), smap),
                  pl.BlockSpec((1, N_HEADS, DK_A), mmap),
                  pl.BlockSpec((1, 1, LANES), mmap),
                  pl.BlockSpec(wout.shape, cmap),
                  pl.BlockSpec(lng.shape, cmap),
                  pl.BlockSpec(lnb.shape, cmap)],
        out_specs=[pl.BlockSpec((tr, D_MODEL), rmap),
                   pl.BlockSpec((1, N_HEADS, DK_A, DV_A), smap),
                   pl.BlockSpec((1, N_HEADS, DK_A), mmap),
                   pl.BlockSpec((1, 1, LANES), mmap)],
        out_shape=[jax.ShapeDtypeStruct((batch * n_t * tr, D_MODEL), F32),
                   jax.ShapeDtypeStruct((batch, N_HEADS, DK_A, DV_A), F32),
                   jax.ShapeDtypeStruct((batch, N_HEADS, DK_A), F32),
                   jax.ShapeDtypeStruct((batch, 1, LANES), F32)],
        scratch_shapes=[pltpu.VMEM((N_HEADS, DK_A, 2 * DV_A), F32),
                        pltpu.VMEM((1, LANES), F32),
                        pltpu.VMEM((tc, V_W), BF16)],
        compiler_params=_cparams(2),
        name="mlstm_scan",
    )(qv, kt, o, gi, gf, gt, x, c0, n0, m0, wout, lng, lnb)


def _mlp_kernel(x_ref, wup_ref, wdn_ref, lng_ref, lnb_ref, y_ref, acc_s, xb_s):
    j = pl.program_id(1)

    @pl.when(j == 0)
    def _():
        xb_s[...] = x_ref[...].astype(BF16)
        acc_s[...] = jnp.zeros_like(acc_s)

    hid = jnp.maximum(_dot(xb_s[...], wup_ref[...]), 0.0)
    acc_s[...] += _dot((hid * hid).astype(BF16), wdn_ref[...])

    @pl.when(j == pl.num_programs(1) - 1)
    def _():
        y_ref[...] = _layer_norm(ALPHA * x_ref[...] + acc_s[...], lng_ref[...], lnb_ref[...])


def _mlp(x, wup, wdn, lng, lnb, tm, tf, name="mlp_ln"):
    rows = x.shape[0]
    return pl.pallas_call(
        _mlp_kernel,
        grid=(rows // tm, D_FF // tf),
        in_specs=[pl.BlockSpec((tm, D_MODEL), lambda i, j: (i, 0)),
                  pl.BlockSpec((D_MODEL, tf), lambda i, j: (0, j)),
                  pl.BlockSpec((tf, D_MODEL), lambda i, j: (j, 0)),
                  pl.BlockSpec(lng.shape, lambda i, j: (0, 0)),
                  pl.BlockSpec(lnb.shape, lambda i, j: (0, 0))],
        out_specs=pl.BlockSpec((tm, D_MODEL), lambda i, j: (i, 0)),
        out_shape=jax.ShapeDtypeStruct((rows, D_MODEL), F32),
        scratch_shapes=[pltpu.VMEM((tm, D_MODEL), F32), pltpu.VMEM((tm, D_MODEL), BF16)],
        compiler_params=_cparams(2),
        name=name,
    )(x, wup, wdn, lng, lnb)


def _qkv_b_kernel(x_ref, wq_ref, wkv_ref, q_ref, k_ref, v_ref, kt_ref, vb_ref, mean_ref, *, tm):
    xh = x_ref[...].astype(BF16)
    step = 512
    for c0 in range(0, D_MODEL, step):
        q_ref[:, c0:c0 + step] = _dot(xh, wq_ref[:, c0:c0 + step]).astype(BF16)
        kk = _dot(xh, wkv_ref[:, c0:c0 + step])
        k_ref[:, c0:c0 + step] = kk
        for hh in range(step // HD_B):
            for r in range(tm // BLOCK):
                kt_ref[c0 // HD_B + hh, r] = kk[r * BLOCK:(r + 1) * BLOCK, hh * HD_B:(hh + 1) * HD_B].T.astype(BF16)
        for r in range(tm // BLOCK):
            mean_ref[r, :, c0:c0 + step] = jnp.mean(kk[r * BLOCK:(r + 1) * BLOCK], axis=0, keepdims=True)
        vv = _dot(xh, wkv_ref[:, D_MODEL + c0:D_MODEL + c0 + step])
        v_ref[:, c0:c0 + step] = vv
        vb_ref[:, c0:c0 + step] = vv.astype(BF16)


def _qkv_b(x, wq, wkv, tm):
    rows = x.shape[0]
    rmap = lambda i: (i, 0)
    rspec = pl.BlockSpec((tm, D_MODEL), rmap)
    return pl.pallas_call(
        functools.partial(_qkv_b_kernel, tm=tm),
        grid=(rows // tm,),
        in_specs=[rspec, pl.BlockSpec(wq.shape, lambda i: (0, 0)), pl.BlockSpec(wkv.shape, lambda i: (0, 0))],
        out_specs=[rspec, rspec, rspec,
                   pl.BlockSpec((N_HEADS, tm // BLOCK, HD_B, BLOCK), lambda i: (0, i, 0, 0)),
                   rspec,
                   pl.BlockSpec((tm // BLOCK, 1, D_MODEL), lambda i: (i, 0, 0))],
        out_shape=[jax.ShapeDtypeStruct((rows, D_MODEL), BF16),
                   jax.ShapeDtypeStruct((rows, D_MODEL), F32),
                   jax.ShapeDtypeStruct((rows, D_MODEL), F32),
                   jax.ShapeDtypeStruct((N_HEADS, rows // BLOCK, HD_B, BLOCK), BF16),
                   jax.ShapeDtypeStruct((rows, D_MODEL), BF16),
                   jax.ShapeDtypeStruct((rows // BLOCK, 1, D_MODEL), F32)],
        compiler_params=_cparams(1),
        name="moba_qkv_proj",
    )(x, wq, wkv)


def _out_ln_kernel(a_ref, w_ref, x_ref, lng_ref, lnb_ref, y_ref):
    y = _dot(a_ref[...].astype(BF16), w_ref[...])
    y_ref[...] = _layer_norm(ALPHA * x_ref[...] + y, lng_ref[...], lnb_ref[...])


def _out_ln(a, w, x, lng, lnb, tm):
    rows = x.shape[0]
    rspec = pl.BlockSpec((tm, D_MODEL), lambda i: (i, 0))
    cmap = lambda i: (0, 0)
    return pl.pallas_call(
        _out_ln_kernel,
        grid=(rows // tm,),
        in_specs=[rspec, pl.BlockSpec(w.shape, cmap), rspec,
                  pl.BlockSpec(lng.shape, cmap), pl.BlockSpec(lnb.shape, cmap)],
        out_specs=rspec,
        out_shape=jax.ShapeDtypeStruct((rows, D_MODEL), F32),
        compiler_params=_cparams(1),
        name="attn_out_ln",
    )(a, w, x, lng, lnb)


def _head_slopes(rep):
    slopes = np.zeros((1, LANES), np.float32)
    slopes[0, :N_HEADS * rep] = np.repeat(2.0 ** -np.arange(1, N_HEADS + 1, dtype=np.float64), rep)
    return jnp.asarray(slopes)


def _head_block_diag(rows_8):
    tiled = jnp.concatenate([rows_8] * (LANES // 8), axis=0)
    r = lax.broadcasted_iota(jnp.int32, (LANES, D_MODEL), 0)
    c = lax.broadcasted_iota(jnp.int32, (LANES, D_MODEL), 1)
    keep = (r // 8) == (c // HD_B)
    return jnp.where(keep, tiled, 0.0).astype(BF16)


def _moba_prompt_kernel(q_ref, kt_ref, vb_ref, mean_ref, slope_ref, x_ref, wout_ref, lng_ref, lnb_ref, o_ref,
                        sel_s, mask_s, m_s, l_s, acc_s, s_s):
    qi = pl.program_id(1)
    scale = HD_B ** -0.5
    rows = BLOCK

    @pl.when(jnp.logical_and(pl.program_id(0) == 0, qi == 0))
    def _():
        t_i = lax.broadcasted_iota(jnp.int32, (BLOCK, BLOCK), 0)
        j_i = lax.broadcasted_iota(jnp.int32, (BLOCK, BLOCK), 1)
        jf = j_i.astype(F32)
        for h in range(N_HEADS):
            mask_s[0, h] = (LOG2E * 2.0 ** -(h + 1)) * jf
            mask_s[1, h] = jnp.where(j_i > t_i, NEG_INF, (LOG2E * 2.0 ** -(h + 1)) * jf)

    mbd = _head_block_diag(mean_ref[0])
    sc = _dot_nt(q_ref[...], mbd)
    lane = lax.broadcasted_iota(jnp.int32, sc.shape, 1)
    blk = lane % 8
    past = jnp.logical_and(lane < N_HEADS * 8, blk < qi)
    sm = jnp.where(past, sc, NEG_INF)
    rank = jnp.zeros(sc.shape, F32)
    for r in range(1, 8):
        lo = pltpu.roll(sm, r, 1)
        rank += jnp.where(blk >= r, jnp.where(lo >= sm, 1.0, 0.0), 0.0)
        hi = pltpu.roll(sm, LANES - r, 1)
        rank += jnp.where(blk + r < 8, jnp.where(hi > sm, 1.0, 0.0), 0.0)
    sel_s[...] = jnp.where(past, jnp.where(rank < TOPK, 1.0, 0.0), 0.0).astype(BF16)
    m_s[...] = jnp.full(m_s.shape, M_FLOOR, F32)
    l_s[...] = jnp.zeros_like(l_s)
    acc_s[...] = jnp.zeros_like(acc_s)

    er = lax.broadcasted_iota(jnp.int32, (LANES, LANES), 0)
    ec = lax.broadcasted_iota(jnp.int32, (LANES, LANES), 1)
    ones_v = jnp.ones((BLOCK, HD_B), BF16)

    def key_block(kn, carry):
        pick = jnp.where(jnp.logical_and(er == ec * 8 + kn, ec < N_HEADS), 1.0, 0.0).astype(BF16)
        selcols = _dot(sel_s[...], pick)
        kbase = slope_ref[...] * (LOG2E * (kn * BLOCK).astype(F32))
        own = kn == qi
        biasc = jnp.where(jnp.logical_or(selcols > 0.5, own), kbase, NEG_INF)
        own_i = own.astype(jnp.int32)
        r0 = pl.multiple_of(kn * BLOCK, BLOCK)

        for h in range(N_HEADS):
            cs = slice(h * HD_B, (h + 1) * HD_B)
            s_s[h] = _dot(q_ref[:, cs], kt_ref[h, kn])

        for h in range(N_HEADS):
            cs = slice(h * HD_B, (h + 1) * HD_B)
            vaug = jnp.concatenate([vb_ref[pl.ds(r0, BLOCK), cs], ones_v], axis=1)
            bias = jnp.broadcast_to(biasc[:, h:h + 1], (rows, LANES))
            lg = s_s[h] * (scale * LOG2E) + mask_s[own_i, h] + jnp.concatenate([bias, bias], axis=1)
            m_old = m_s[h]
            m_new = jnp.maximum(m_old, jnp.max(lg, axis=1, keepdims=True))
            alpha = jnp.exp2(m_old - m_new)
            pr = jnp.exp2(lg - jnp.concatenate([m_new, m_new], axis=1))
            pv = _dot(pr.astype(BF16), vaug)
            l_s[h] = alpha * l_s[h] + pv[:, HD_B:]
            acc_s[:, cs] = alpha * acc_s[:, cs] + pv[:, :HD_B]
            m_s[h] = m_new
        return carry

    lax.fori_loop(0, qi + 1, key_block, 0)

    for h in range(N_HEADS):
        cs = slice(h * HD_B, (h + 1) * HD_B)
        acc_s[:, cs] = acc_s[:, cs] / l_s[h]
    y = _dot(acc_s[...].astype(BF16), wout_ref[...])
    o_ref[...] = _layer_norm(ALPHA * x_ref[...] + y, lng_ref[...], lnb_ref[...])


def _moba_prompt(q, kt, vb, means, x, wout, lng, lnb, *, batch, seq):
    nb = seq // BLOCK
    qmap = lambda b, i: (b * nb + i, 0)
    cmap = lambda b, i: (0, 0)
    return pl.pallas_call(
        _moba_prompt_kernel,
        grid=(batch, nb),
        in_specs=[pl.BlockSpec((BLOCK, D_MODEL), qmap),
                  pl.BlockSpec((N_HEADS, nb, HD_B, BLOCK), lambda b, i: (0, b, 0, 0)),
                  pl.BlockSpec((seq, D_MODEL), lambda b, i: (b, 0)),
                  pl.BlockSpec((1, nb, D_MODEL), lambda b, i: (b, 0, 0)),
                  pl.BlockSpec((1, LANES), cmap),
                  pl.BlockSpec((BLOCK, D_MODEL), qmap),
                  pl.BlockSpec(wout.shape, cmap),
                  pl.BlockSpec(lng.shape, cmap),
                  pl.BlockSpec(lnb.shape, cmap)],
        out_specs=pl.BlockSpec((BLOCK, D_MODEL), qmap),
        out_shape=jax.ShapeDtypeStruct((batch * seq, D_MODEL), F32),
        scratch_shapes=[pltpu.VMEM((BLOCK, LANES), BF16),
                        pltpu.VMEM((2, N_HEADS, BLOCK, BLOCK), F32),
                        pltpu.VMEM((N_HEADS, BLOCK, LANES), F32),
                        pltpu.VMEM((N_HEADS, BLOCK, LANES), F32),
                        pltpu.VMEM((BLOCK, D_MODEL), F32),
                        pltpu.VMEM((N_HEADS, BLOCK, BLOCK), F32)],
        compiler_params=_cparams(2),
        name="moba_prompt_attn",
    )(q, kt, vb, means, _head_slopes(1), x, wout, lng, lnb)


PAGES_PER_STEP = 16


def _moba_sample_kernel(ktab_ref, vtab_ref, q_ref, kn_ref, vn_ref, *refs, n_pages, dec_seq):
    kp_refs = refs[:PAGES_PER_STEP]
    vp_refs = refs[PAGES_PER_STEP:2 * PAGES_PER_STEP]
    o_ref = refs[2 * PAGES_PER_STEP]
    lt_s, po_s, rden_s, acc_s = refs[2 * PAGES_PER_STEP + 1:]
    ph = pl.program_id(1)
    s = pl.program_id(2)
    n_steps = pl.num_programs(2)
    past_len = n_pages * PAGE
    n_blocks = past_len // BLOCK
    ppb = BLOCK // PAGE
    scale = HD_B ** -0.5

    def head_rows(page_ref, h):
        return page_ref[0, pl.ds(h, PAGE, stride=N_HEADS), :]

    def pad_page(new_ref, h):
        blk = new_ref[:, h * HD_B:(h + 1) * HD_B]
        return jnp.concatenate([blk, jnp.zeros((PAGE - dec_seq, HD_B), F32)], axis=0).astype(BF16)

    def q_head(h):
        return q_ref[:, h * HD_B:(h + 1) * HD_B].astype(BF16)

    @pl.when(ph == 0)
    def _():
        for i in range(PAGES_PER_STEP):
            pg = s * PAGES_PER_STEP + i
            for h in range(N_HEADS):
                k_t = head_rows(kp_refs[i], h).T.astype(BF16)
                lt_s[h, pg] = _dot(q_head(h), k_t)

    @pl.when(jnp.logical_and(ph == 0, s == n_steps - 1))
    def _():
        lane = lax.broadcasted_iota(jnp.int32, (dec_seq, LANES), 1)
        rows = []
        for h in range(N_HEADS):
            blk_sum = jnp.sum(lt_s[h].reshape(n_blocks, ppb, dec_seq, PAGE), axis=1)
            tot = jnp.sum(blk_sum, axis=2, keepdims=True) * (1.0 / BLOCK)
            sc_h = jnp.full((dec_seq, LANES), NEG_INF, F32)
            for n in range(n_blocks):
                sc_h = jnp.where(lane == n, tot[n], sc_h)
            rows.append(sc_h)
        sc = jnp.concatenate(rows, axis=0)
        lane_a = lax.broadcasted_iota(jnp.int32, sc.shape, 1)
        rank = jnp.zeros(sc.shape, F32)
        for r in range(1, n_blocks):
            lo = pltpu.roll(sc, r, 1)
            rank += jnp.where(lane_a >= r, jnp.where(lo >= sc, 1.0, 0.0), 0.0)
            hi = pltpu.roll(sc, LANES - r, 1)
            rank += jnp.where(lane_a + r < n_blocks, jnp.where(hi > sc, 1.0, 0.0), 0.0)
        sel_bias = jnp.where(jnp.logical_and(rank < TOPK, lane_a < n_blocks), 0.0, NEG_INF)
        kpos = (lax.broadcasted_iota(jnp.int32, (n_pages, 1, PAGE), 0) * PAGE
                + lax.broadcasted_iota(jnp.int32, (n_pages, 1, PAGE), 2) - past_len).astype(F32)
        qi = lax.broadcasted_iota(jnp.int32, (dec_seq, PAGE), 0)
        ki = lax.broadcasted_iota(jnp.int32, (dec_seq, PAGE), 1)
        for h in range(N_HEADS):
            slope = 2.0 ** -(h + 1)
            bias_h = sel_bias[h * dec_seq:(h + 1) * dec_seq, :]
            bias_pages = jnp.stack([jnp.broadcast_to(bias_h[:, n:n + 1], (dec_seq, PAGE))
                                    for n in range(n_blocks) for _ in range(ppb)], axis=0)
            lgs = lt_s[h] * scale + slope * kpos + bias_pages
            lgo = _dot_nt(q_head(h), pad_page(kn_ref, h))
            lgo = jnp.where(ki <= qi, lgo * scale + slope * ki.astype(F32), NEG_INF)
            mx = jnp.maximum(jnp.max(jnp.max(lgs, axis=0), axis=1, keepdims=True),
                             jnp.max(lgo, axis=1, keepdims=True))
            pr = jnp.exp(lgs - mx[None])
            po = jnp.exp(lgo - mx)
            den = jnp.sum(jnp.sum(pr, axis=0), axis=1, keepdims=True) + jnp.sum(po, axis=1, keepdims=True)
            lt_s[h] = pr
            po_s[h] = po
            rden_s[h] = jnp.broadcast_to(1.0 / den, (dec_seq, LANES))

    @pl.when(jnp.logical_and(ph == 1, s == 0))
    def _():
        for h in range(N_HEADS):
            acc_s[:, h * HD_B:(h + 1) * HD_B] = _dot(po_s[h].astype(BF16), pad_page(vn_ref, h))

    @pl.when(ph == 1)
    def _():
        for i in range(PAGES_PER_STEP):
            pg = s * PAGES_PER_STEP + i
            for h in range(N_HEADS):
                acc_s[:, h * HD_B:(h + 1) * HD_B] += _dot(lt_s[h, pg].astype(BF16),
                                                          head_rows(vp_refs[i], h).astype(BF16))

    @pl.when(jnp.logical_and(ph == 1, s == n_steps - 1))
    def _():
        for h in range(N_HEADS):
            cs = slice(h * HD_B, (h + 1) * HD_B)
            o_ref[:, cs] = acc_s[:, cs] * rden_s[h]


def _moba_sample(page_table, q, k_new, v_new, cache_k, cache_v, *, batch, dec_seq):
    n_pages = page_table.shape[1]
    n_steps = n_pages // PAGES_PER_STEP
    rspec = pl.BlockSpec((dec_seq, D_MODEL), lambda b, ph, s, kt, vt: (b, 0))
    pt3 = page_table.reshape(batch, n_steps, PAGES_PER_STEP)
    last = jnp.broadcast_to(pt3[:, n_steps - 1:n_steps], pt3.shape)
    prev_last = jnp.concatenate([last[:1], last[:-1]], axis=0)
    k_tab = jnp.stack([pt3, last], axis=1).reshape(batch * 2 * n_steps, PAGES_PER_STEP)
    v_tab = jnp.stack([prev_last, pt3], axis=1).reshape(batch * 2 * n_steps, PAGES_PER_STEP)

    def kmap(i):
        return lambda b, ph, s, kt, vt: (kt[(b * 2 + ph) * n_steps + s, i], 0, 0)

    def vmap_(i):
        return lambda b, ph, s, kt, vt: (vt[(b * 2 + ph) * n_steps + s, i], 0, 0)

    pspec = lambda m: pl.BlockSpec((1, PAGE * N_HEADS, HD_B), m)
    grid_spec = pltpu.PrefetchScalarGridSpec(
        num_scalar_prefetch=2,
        grid=(batch, 2, n_steps),
        in_specs=[rspec, rspec, rspec]
                 + [pspec(kmap(i)) for i in range(PAGES_PER_STEP)]
                 + [pspec(vmap_(i)) for i in range(PAGES_PER_STEP)],
        out_specs=rspec,
        scratch_shapes=[pltpu.VMEM((N_HEADS, n_pages, dec_seq, PAGE), F32),
                        pltpu.VMEM((N_HEADS, dec_seq, PAGE), F32),
                        pltpu.VMEM((N_HEADS, dec_seq, LANES), F32),
                        pltpu.VMEM((dec_seq, D_MODEL), F32)])
    kern = functools.partial(_moba_sample_kernel, n_pages=n_pages, dec_seq=dec_seq)
    return pl.pallas_call(
        kern,
        grid_spec=grid_spec,
        out_shape=jax.ShapeDtypeStruct((batch * dec_seq, D_MODEL), F32),
        compiler_params=_cparams(3),
        name="moba_sample_attn",
    )(k_tab, v_tab, q, k_new, v_new, *([cache_k] * PAGES_PER_STEP), *([cache_v] * PAGES_PER_STEP))


def kernel(x_prompt, x_sample, state_C, state_n, state_m, cache_k, cache_v, page_table,
           w_in_a, b_gate_a, w_out_a, w_q_b, w_out_b, w_kv, w_up, w_down, ln_g, ln_b):
    bp, sp, _ = x_prompt.shape
    bs, ss, _ = x_sample.shape
    ch = MLSTM_CHUNK
    nq = 2 * QK_W + 2 * V_W

    wm_a = w_in_a[0].astype(BF16)
    wg_a = jnp.pad(w_in_a[0, :, nq:], ((0, 0), (0, LANES - 2 * N_HEADS)))
    bg_a = jnp.pad(b_gate_a[0], (0, LANES - 2 * N_HEADS)).reshape(1, LANES)
    wout_a = w_out_a[0].astype(BF16)
    wq_b = w_q_b[0].astype(BF16)
    wkv_b = w_kv.astype(BF16)
    wout_b = w_out_b[0].astype(BF16)
    wup = w_up.astype(BF16)
    wdn = w_down.astype(BF16)
    lng = ln_g.reshape(DEPTH, 2, 1, D_MODEL)
    lnb = ln_b.reshape(DEPTH, 2, 1, D_MODEL)

    xp = x_prompt.reshape(bp * sp, D_MODEL)
    qv, kt, o, gi, gf, gt = _proj_a(xp, wm_a, wg_a, bg_a, 1024, ch)
    c0 = jnp.zeros((bp, N_HEADS, DK_A, DV_A), F32)
    n0 = jnp.zeros((bp, N_HEADS, DK_A), F32)
    m0 = jnp.zeros((bp, 1, LANES), F32)
    x1, c_p, n_p, m_p = _mlstm(qv, kt, o, gi, gf, gt, xp, c0, n0, m0, wout_a, lng[0, 0], lnb[0, 0],
                             batch=bp, seq=sp, tc=1024, ch=ch)
    x2 = _mlp(x1, wup[0], wdn[0], lng[0, 1], lnb[0, 1], 1024, 2048)
    q, k, v, kt, vb, means = _qkv_b(x2, wq_b, wkv_b, 512)
    x3 = _moba_prompt(q, kt, vb, means.reshape(bp, sp // BLOCK, D_MODEL), x2, wout_b, lng[1, 0], lnb[1, 0],
                      batch=bp, seq=sp)
    y_prompt = _mlp(x3, wup[1], wdn[1], lng[1, 1], lnb[1, 1], 1024, 2048)

    rows_s = bs * ss
    xs = x_sample.reshape(rows_s, D_MODEL)
    qv_s, kt_s, o_s, gi_s, gf_s, _ = _proj_a(xs, wm_a, wg_a, bg_a, rows_s, ch)
    pad3 = lambda a, val=0.0: jnp.pad(a.reshape(bs, ss, -1), ((0, 0), (0, ch - ss), (0, 0)),
                                      constant_values=val)
    flat = lambda a: a.reshape(bs * ch, -1)
    gi_pad = jnp.where(jnp.arange(LANES)[None, None, :] < N_HEADS, pad3(gi_s, NEG_INF), 0.0)
    gf_pad = pad3(gf_s)
    gt_pad = jnp.concatenate([jnp.swapaxes(gi_pad, 1, 2)[:, :N_HEADS],
                              jnp.swapaxes(gf_pad, 1, 2)[:, :N_HEADS]], axis=1)
    m0s = jnp.pad(state_m[0], ((0, 0), (0, LANES - N_HEADS))).reshape(bs, 1, LANES)
    kt_pad = jnp.pad(kt_s.reshape(QK_W, bs, ss), ((0, 0), (0, 0), (0, ch - ss))).reshape(QK_W, bs * ch)
    x1s, c_s, n_s, m_s = _mlstm(qv_s.astype(F32), kt_pad, o_s, flat(gi_pad), flat(gf_pad), gt_pad,
                                xs, state_C[0], state_n[0], m0s, wout_a, lng[0, 0], lnb[0, 0],
                                batch=bs, seq=ch, tc=ch, ch=ch, valid=ss)
    x2s = _mlp(x1s, wup[0], wdn[0], lng[0, 1], lnb[0, 1], rows_s, 512)
    q_s, k_s, v_s, _, _, _ = _qkv_b(x2s, wq_b, wkv_b, rows_s)
    n_pool = cache_k.shape[0]
    att_s = _moba_sample(page_table, q_s.astype(F32), k_s, v_s,
                         cache_k.reshape(n_pool, PAGE * N_HEADS, HD_B),
                         cache_v.reshape(n_pool, PAGE * N_HEADS, HD_B), batch=bs, dec_seq=ss)
    x3s = _out_ln(att_s, wout_b, x2s, lng[1, 0], lnb[1, 0], rows_s)
    y_sample = _mlp(x3s, wup[1], wdn[1], lng[1, 1], lnb[1, 1], rows_s, 512)

    p_c, p_n, p_m = c_p[None], n_p[None], m_p[None, :, 0, :N_HEADS]
    s_c, s_n, s_m = c_s[None], n_s[None], m_s[None, :, 0, :N_HEADS]
    return (y_prompt.reshape(bp, sp, D_MODEL), y_sample.reshape(bs, ss, D_MODEL),
            p_c, p_n, p_m,
            k.reshape(bp, sp, N_HEADS, HD_B), v.reshape(bp, sp, N_HEADS, HD_B),
            s_c, s_n, s_m,
            k_s.reshape(bs, ss, N_HEADS, HD_B), v_s.reshape(bs, ss, N_HEADS, HD_B))
```
